```python
import jax, jax.numpy as jnp
from jax import lax
import numpy as np

D_MODEL = 1024
BATCH = 8
SEQ = 4096
DEPTH = 4

N_MIXERS = 2
N_MLSTM_LAYERS = (DEPTH + 1) // 2
N_CONV_LAYERS = DEPTH // 2
MLSTM_INNER = 2 * D_MODEL
MLSTM_HEADS = 4
MLSTM_HEAD_DIM = MLSTM_INNER // MLSTM_HEADS
QKV_BLOCK = 4
MLSTM_CONV = 4
CHUNK = 64
CONF_KERNEL = 31
D_FF = 2816
FFN_CONV = 3
EPS = 1e-6

kernel_name = "hybrid_mlstm_conformer_convffn_adaln"


def rmsnorm(x, g):
    xf = x.astype(jnp.float32)
    y = xf * lax.rsqrt(jnp.mean(xf * xf, axis=-1, keepdims=True) + EPS)
    return (y * g).astype(x.dtype)


def layernorm(x, g, b):
    xf = x.astype(jnp.float32)
    mu = jnp.mean(xf, axis=-1, keepdims=True)
    var = jnp.mean(jnp.square(xf - mu), axis=-1, keepdims=True)
    return ((xf - mu) * lax.rsqrt(var + EPS) * g + b).astype(x.dtype)


def modulate(h, shift, scale):
    return h * (1.0 + scale[:, None, :]) + shift[:, None, :]


def causal_dwconv(x, w, b):
    K, C = w.shape
    y = lax.conv_general_dilated(
        x, w[:, None, :].astype(x.dtype), window_strides=(1,), padding=[(K - 1, 0)],
        dimension_numbers=("NWC", "WIO", "NWC"), feature_group_count=C)
    return y + b


def blockdiag(x, w):
    B, S, I = x.shape
    xb = x.reshape(B, S, I // QKV_BLOCK, QKV_BLOCK)
    return jnp.einsum("bsni,nio->bsno", xb, w).reshape(B, S, I)


def mlstm_chunkwise(q, k, v, i_pre, f_pre):
    B, H, S, dh = q.shape
    nc = S // CHUNK
    k = k * (dh ** -0.5)
    logf = jax.nn.log_sigmoid(f_pre)

    def chunked(t):
        return jnp.moveaxis(t.reshape(B, H, nc, CHUNK, *t.shape[3:]), 2, 0)

    causal = jnp.tril(jnp.ones((CHUNK, CHUNK), dtype=bool))

    def step(carry, inp):
        C, n, m = carry
        q_, k_, v_, i_, lf = inp
        g = jnp.cumsum(lf, axis=-1)
        G = g[..., -1]
        D = g[..., :, None] - g[..., None, :] + i_[..., None, :]
        D = jnp.where(causal, D, -jnp.inf)
        inter = m[..., None] + g
        m_row = jnp.maximum(inter, jnp.max(D, axis=-1))
        w_intra = jnp.exp(D - m_row[..., None])
        w_inter = jnp.exp(inter - m_row)
        s = jnp.einsum("bhld,bhsd->bhls", q_, k_) * w_intra
        num = (w_inter[..., None] * jnp.einsum("bhvk,bhlk->bhlv", C, q_)
               + jnp.einsum("bhls,bhsv->bhlv", s, v_))
        den = w_inter * jnp.einsum("bhk,bhlk->bhl", n, q_) + jnp.sum(s, axis=-1)
        h = num / jnp.maximum(jnp.abs(den), jnp.exp(-m_row))[..., None]
        decay = G[..., None] - g + i_
        m_new = jnp.maximum(m + G, jnp.max(decay, axis=-1))
        w_s = jnp.exp(decay - m_new[..., None])
        w_c = jnp.exp(m + G - m_new)
        C_new = w_c[..., None, None] * C + jnp.einsum("bhs,bhsv,bhsk->bhvk", w_s, v_, k_)
        n_new = w_c[..., None] * n + jnp.einsum("bhs,bhsk->bhk", w_s, k_)
        return (C_new, n_new, m_new), h

    init = (jnp.zeros((B, H, dh, dh), jnp.float32),
            jnp.zeros((B, H, dh), jnp.float32),
            jnp.zeros((B, H), jnp.float32))
    _, hs = lax.scan(step, init, (chunked(q), chunked(k), chunked(v), chunked(i_pre), chunked(logf)))
    hs = jnp.moveaxis(hs, 0, 2).reshape(B, H, S, dh)
    return hs.astype(v.dtype)


def mlstm_block(h, w_up, conv_w, conv_b, wq, wk, wv, w_if, b_if, ln_w, skip, w_down):
    B, S, _ = h.shape
    H, dh = MLSTM_HEADS, MLSTM_HEAD_DIM
    up = h @ w_up
    xm, z = jnp.split(up, 2, axis=-1)
    xc = jax.nn.silu(causal_dwconv(xm, conv_w, conv_b))
    q = blockdiag(xc, wq)
    k = blockdiag(xc, wk)
    v = blockdiag(xm, wv)
    gates = (jnp.concatenate([q, k, v], axis=-1) @ w_if + b_if).astype(jnp.float32)
    i_pre = jnp.transpose(gates[..., :H], (0, 2, 1))
    f_pre = jnp.transpose(gates[..., H:], (0, 2, 1))

    def heads(t):
        return jnp.transpose(t.reshape(B, S, H, dh), (0, 2, 1, 3))

    hc = mlstm_chunkwise(heads(q), heads(k), heads(v), i_pre, f_pre)
    hf = hc.astype(jnp.float32)
    mu = jnp.mean(hf, axis=-1, keepdims=True)
    var = jnp.mean(jnp.square(hf - mu), axis=-1, keepdims=True)
    hn = ((hf - mu) * lax.rsqrt(var + EPS)).astype(h.dtype)
    hn = jnp.transpose(hn, (0, 2, 1, 3)).reshape(B, S, MLSTM_INNER) * ln_w
    out = (hn + skip * xc) * jax.nn.silu(z)
    return out @ w_down


def conformer_conv_block(h, w_pw1, b_pw1, dw_w, dw_b, ln_g, ln_b, w_pw2, b_pw2):
    a, gte = jnp.split(h @ w_pw1 + b_pw1, 2, axis=-1)
    u = a * jax.nn.sigmoid(gte)
    u = causal_dwconv(u, dw_w, dw_b)
    u = jax.nn.silu(layernorm(u, ln_g, ln_b))
    return u @ w_pw2 + b_pw2


def conv_ffn(h, w_up, conv_w, conv_b, w_down):
    u = causal_dwconv(h @ w_up, conv_w, conv_b)
    gate, val = jnp.split(u, 2, axis=-1)
    return (jax.nn.silu(gate) * val) @ w_down


def setup_inputs(seed: int = 0) -> dict:
    key = jax.random.key(seed)
    ks = iter(jax.random.split(key, 40))
    D, I, H, F = D_MODEL, MLSTM_INNER, MLSTM_HEADS, D_FF
    NA, NB = N_MLSTM_LAYERS, N_CONV_LAYERS

    def nrm(shape, scale):
        return jax.random.normal(next(ks), shape, jnp.float32) * scale

    def gain(shape):
        return 1.0 + nrm(shape, 0.02)

    f_bias = jnp.broadcast_to(jnp.linspace(3.0, 6.0, H, dtype=jnp.float32), (NA, H))
    b_if = jnp.concatenate([nrm((NA, H), 0.1), f_bias + nrm((NA, H), 0.1)], axis=-1)
    return {
        "x": nrm((BATCH, SEQ, D), 1.0),
        "c": nrm((BATCH, D), 1.0),
        "ada_w": nrm((DEPTH, D, 6 * D), D ** -0.5),
        "ada_b": nrm((DEPTH, 6 * D), 0.02),
        "norm_mix_g": gain((DEPTH, D)),
        "norm_ffn_g": gain((DEPTH, D)),
        "final_g": gain((D,)),
        "mlstm_w_up": nrm((NA, D, 2 * I), D ** -0.5),
        "mlstm_conv_w": nrm((NA, MLSTM_CONV, I), MLSTM_CONV ** -0.5),
        "mlstm_conv_b": nrm((NA, I), 0.02),
        "mlstm_wq": nrm((NA, I // QKV_BLOCK, QKV_BLOCK, QKV_BLOCK), QKV_BLOCK ** -0.5),
        "mlstm_wk": nrm((NA, I // QKV_BLOCK, QKV_BLOCK, QKV_BLOCK), QKV_BLOCK ** -0.5),
        "mlstm_wv": nrm((NA, I // QKV_BLOCK, QKV_BLOCK, QKV_BLOCK), QKV_BLOCK ** -0.5),
        "mlstm_w_if": nrm((NA, 3 * I, 2 * H), (3 * I) ** -0.5),
        "mlstm_b_if": b_if,
        "mlstm_ln_w": gain((NA, I)),
        "mlstm_skip": gain((NA, I)),
        "mlstm_w_down": nrm((NA, I, D), I ** -0.5),
        "conf_w_pw1": nrm((NB, D, 2 * D), D ** -0.5),
        "conf_b_pw1": nrm((NB, 2 * D), 0.02),
        "conf_dw_w": nrm((NB, CONF_KERNEL, D), CONF_KERNEL ** -0.5),
        "conf_dw_b": nrm((NB, D), 0.02),
        "conf_ln_g": gain((NB, D)),
        "conf_ln_b": nrm((NB, D), 0.02),
        "conf_w_pw2": nrm((NB, D, D), D ** -0.5),
        "conf_b_pw2": nrm((NB, D), 0.02),
        "ffn_w_up": nrm((DEPTH, D, 2 * F), D ** -0.5),
        "ffn_conv_w": nrm((DEPTH, FFN_CONV, 2 * F), FFN_CONV ** -0.5),
        "ffn_conv_b": nrm((DEPTH, 2 * F), 0.02),
        "ffn_w_down": nrm((DEPTH, F, D), F ** -0.5),
    }


def reference(x, c, ada_w, ada_b, norm_mix_g, norm_ffn_g, final_g,
              mlstm_w_up, mlstm_conv_w, mlstm_conv_b, mlstm_wq, mlstm_wk, mlstm_wv,
              mlstm_w_if, mlstm_b_if, mlstm_ln_w, mlstm_skip, mlstm_w_down,
              conf_w_pw1, conf_b_pw1, conf_dw_w, conf_dw_b, conf_ln_g, conf_ln_b,
              conf_w_pw2, conf_b_pw2,
              ffn_w_up, ffn_conv_w, ffn_conv_b, ffn_w_down):
    c_act = jax.nn.silu(c)
    for layer in range(DEPTH):
        mod = c_act @ ada_w[layer] + ada_b[layer]
        sh1, sc1, g1, sh2, sc2, g2 = jnp.split(mod, 6, axis=-1)
        h = modulate(rmsnorm(x, norm_mix_g[layer]), sh1, sc1)
        j = layer // N_MIXERS
        if layer % N_MIXERS == 0:
            y = mlstm_block(h, mlstm_w_up[j], mlstm_conv_w[j], mlstm_conv_b[j],
                            mlstm_wq[j], mlstm_wk[j], mlstm_wv[j], mlstm_w_if[j], mlstm_b_if[j],
                            mlstm_ln_w[j], mlstm_skip[j], mlstm_w_down[j])
        else:
            y = conformer_conv_block(h, conf_w_pw1[j], conf_b_pw1[j], conf_dw_w[j], conf_dw_b[j],
                                     conf_ln_g[j], conf_ln_b[j], conf_w_pw2[j], conf_b_pw2[j])
        x = x + g1[:, None, :] * y
        h = modulate(rmsnorm(x, norm_ffn_g[layer]), sh2, sc2)
        x = x + g2[:, None, :] * conv_ffn(h, ffn_w_up[layer], ffn_conv_w[layer],
                                          ffn_conv_b[layer], ffn_w_down[layer])
    return rmsnorm(x, final_g)
```

```python
import functools

import jax
import jax.numpy as jnp
from jax import lax
from jax.experimental import pallas as pl
from jax.experimental.pallas import tpu as pltpu

F32 = jnp.float32
BF16 = jnp.bfloat16
EPS = 1e-6

MLSTM_HEADS = 4
QKV_BLOCK = 4
GATE_LANES = 128
BD_GROUP = 256

TM_FFN = 512
TM_CONF = 256
TM_MFRONT = 256
CHUNK_LEN = 256

VMEM_LIMIT_BYTES = 56 * 1024 * 1024


def _params(n_axes):
    return pltpu.CompilerParams(
        dimension_semantics=("arbitrary",) * n_axes,
        vmem_limit_bytes=VMEM_LIMIT_BYTES,
    )


def _resident(shape):
    zeros = (0,) * len(shape)
    return pl.BlockSpec(shape, lambda *_: zeros, pipeline_mode=pl.Buffered(1))


def _rmsnorm(x, g):
    return x * lax.rsqrt(jnp.mean(x * x, axis=-1, keepdims=True) + EPS) * g


def _norm_modulate(x, g, shift, scale):
    return _rmsnorm(x, g) * (1.0 + scale) + shift


def _causal_conv_rows(u, prev8, taps, bias):
    top = u[0:8]
    row8 = lax.broadcasted_iota(jnp.int32, top.shape, 0)
    y = taps[0] * u + bias
    y_top = taps[0] * top + bias
    for j in range(1, len(taps)):
        y = y + taps[j] * pltpu.roll(u, j, 0)
        shifted_top = jnp.where(row8 < j, pltpu.roll(prev8, j, 0), pltpu.roll(top, j, 0))
        y_top = y_top + taps[j] * shifted_top
    return jnp.concatenate([y_top, y[8:]], axis=0)


def _mods_kernel(c_ref, w_ref, b_ref, o_ref):
    c = c_ref[...]
    c_act = (c * jax.nn.sigmoid(c)).astype(BF16)
    o_ref[0] = jnp.dot(c_act, w_ref[0].astype(BF16), preferred_element_type=F32) + b_ref[0]


def _mods_call(c, ada_w, ada_b):
    depth, d, n = ada_w.shape
    b = c.shape[0]
    tn = 1024
    return pl.pallas_call(
        _mods_kernel,
        grid=(depth, n // tn),
        in_specs=[
            pl.BlockSpec((b, d), lambda l, j: (0, 0)),
            pl.BlockSpec((1, d, tn), lambda l, j: (l, 0, j)),
            pl.BlockSpec((1, 1, tn), lambda l, j: (l, 0, j)),
        ],
        out_specs=pl.BlockSpec((1, b, tn), lambda l, j: (l, 0, j)),
        out_shape=jax.ShapeDtypeStruct((depth, b, n), F32),
        compiler_params=_params(2),
        name="adaln_mods",
    )(c, ada_w, ada_b.reshape(depth, 1, n))


def _ffn_kernel(x_ref, mod_ref, ng_ref, wup_ref, cw_ref, cb_ref, wdn_ref, *rest, tm, f_dim, fc,
                final):
    if final:
        fg_ref, o_ref, h_scr, act_scr, halo_scr = rest
    else:
        o_ref, h_scr, act_scr, halo_scr = rest

    @pl.when(pl.program_id(1) == 0)
    def _():
        halo_scr[...] = jnp.zeros_like(halo_scr)

    x = x_ref[0]
    shift, scale, gate = mod_ref[0, 3:4, :], mod_ref[0, 4:5, :], mod_ref[0, 5:6, :]
    h_scr[...] = _norm_modulate(x, ng_ref[...], shift, scale).astype(BF16)

    for f in range(f_dim // fc):
        halves = []
        for half in range(2):
            c0 = half * f_dim + f * fc
            cols = slice(c0, c0 + fc)
            u = jnp.dot(h_scr[...], wup_ref[:, cols], preferred_element_type=F32)
            prev8 = halo_scr[:, cols]
            halo_scr[:, cols] = u[tm - 8:tm]
            taps = (cw_ref[2:3, cols], cw_ref[1:2, cols], cw_ref[0:1, cols])
            halves.append(_causal_conv_rows(u, prev8, taps, cb_ref[0:1, cols]))
        yg, yv = halves
        act_scr[:, f * fc:(f + 1) * fc] = (yg * jax.nn.sigmoid(yg) * yv).astype(BF16)

    y = jnp.dot(act_scr[...], wdn_ref[...], preferred_element_type=F32)
    out = x + gate * y
    if final:
        out = _rmsnorm(out, fg_ref[...])
    o_ref[0] = out


def _ffn_call(x, mod, norm_g, w_up, conv_w, conv_b, w_down, final_g=None):
    b, s, d = x.shape
    f_dim = w_down.shape[0]
    tm, fc = TM_FFN, 256
    assert s % tm == 0 and f_dim % fc == 0
    final = final_g is not None
    row_spec = pl.BlockSpec((1, tm, d), lambda i, j: (i, j, 0))
    in_specs = [
        row_spec,
        pl.BlockSpec((1, 6, d), lambda i, j: (i, 0, 0)),
        _resident((1, d)),
        _resident((d, 2 * f_dim)),
        _resident((3, 2 * f_dim)),
        _resident((1, 2 * f_dim)),
        _resident((f_dim, d)),
    ]
    args = [x, mod, norm_g.reshape(1, d), w_up.astype(BF16), conv_w, conv_b.reshape(1, -1),
            w_down.astype(BF16)]
    if final:
        in_specs.append(_resident((1, d)))
        args.append(final_g.reshape(1, d))
    return pl.pallas_call(
        functools.partial(_ffn_kernel, tm=tm, f_dim=f_dim, fc=fc, final=final),
        grid=(b, s // tm),
        in_specs=in_specs,
        out_specs=row_spec,
        out_shape=jax.ShapeDtypeStruct((b, s, d), F32),
        scratch_shapes=[
            pltpu.VMEM((tm, d), BF16),
            pltpu.VMEM((tm, f_dim), BF16),
            pltpu.VMEM((8, 2 * f_dim), F32),
        ],
        compiler_params=_params(2),
        name="conv_ffn",
    )(*args)


CONF_PAD = 32


def _conf_kernel(x_ref, mod_ref, ng_ref, w1_ref, b1_ref, dw_ref, db_ref, lg_ref, lb_ref, w2_ref,
                 b2_ref, o_ref, ext_scr, shift_scr, y_scr, *, tm, d, kw, rb, cw):
    @pl.when(pl.program_id(1) == 0)
    def _():
        ext_scr[0:CONF_PAD] = jnp.zeros((CONF_PAD, d), F32)

    x = x_ref[0]
    shift, scale, gate = mod_ref[0, 0:1, :], mod_ref[0, 1:2, :], mod_ref[0, 2:3, :]
    h = _norm_modulate(x, ng_ref[...], shift, scale).astype(BF16)
    a = jnp.dot(h, w1_ref[:, 0:d], preferred_element_type=F32) + b1_ref[0:1, 0:d]
    g = jnp.dot(h, w1_ref[:, d:2 * d], preferred_element_type=F32) + b1_ref[0:1, d:2 * d]
    ext_scr[CONF_PAD:CONF_PAD + tm] = a * jax.nn.sigmoid(g)

    span = tm + CONF_PAD - 8
    for b in range(1, 8):
        shift_scr[b - 1] = ext_scr[pl.ds(b, span)]

    base = CONF_PAD - (kw - 1)

    def row_block(i, carry):
        r0 = pl.multiple_of(i * rb, rb)
        for c in range(d // cw):
            cols = slice(c * cw, (c + 1) * cw)
            acc = jnp.broadcast_to(db_ref[0:1, cols], (rb, cw))
            for k in range(kw):
                q8, b = divmod(base + k, 8)
                if b == 0:
                    win = ext_scr[pl.ds(r0 + 8 * q8, rb), cols]
                else:
                    win = shift_scr[b - 1, pl.ds(r0 + 8 * q8, rb), cols]
                acc = acc + dw_ref[k:k + 1, cols] * win
            y_scr[pl.ds(r0, rb), cols] = acc
        return carry

    lax.fori_loop(0, tm // rb, row_block, 0)
    ext_scr[0:CONF_PAD] = ext_scr[tm:tm + CONF_PAD]

    y = y_scr[...]
    mu = jnp.mean(y, axis=-1, keepdims=True)
    var = jnp.mean(jnp.square(y - mu), axis=-1, keepdims=True)
    ln = (y - mu) * lax.rsqrt(var + EPS) * lg_ref[...] + lb_ref[...]
    act = (ln * jax.nn.sigmoid(ln)).astype(BF16)
    out = jnp.dot(act, w2_ref[...], preferred_element_type=F32) + b2_ref[...]
    o_ref[0] = x + gate * out


def _conf_call(x, mod, norm_g, w_pw1, b_pw1, dw_w, dw_b, ln_g, ln_b, w_pw2, b_pw2):
    b, s, d = x.shape
    kw = dw_w.shape[0]
    tm = TM_CONF
    assert s % tm == 0 and kw - 1 <= CONF_PAD
    row_spec = pl.BlockSpec((1, tm, d), lambda i, j: (i, j, 0))
    return pl.pallas_call(
        functools.partial(_conf_kernel, tm=tm, d=d, kw=kw, rb=32, cw=256),
        grid=(b, s // tm),
        in_specs=[
            row_spec,
            pl.BlockSpec((1, 6, d), lambda i, j: (i, 0, 0)),
            _resident((1, d)),
            _resident((d, 2 * d)),
            _resident((1, 2 * d)),
            _resident((kw, d)),
            _resident((1, d)),
            _resident((1, d)),
            _resident((1, d)),
            _resident((d, d)),
            _resident((1, d)),
        ],
        out_specs=row_spec,
        out_shape=jax.ShapeDtypeStruct((b, s, d), F32),
        scratch_shapes=[
            pltpu.VMEM((tm + CONF_PAD, d), F32),
            pltpu.VMEM((7, tm + CONF_PAD - 8, d), F32),
            pltpu.VMEM((tm, d), F32),
        ],
        compiler_params=_params(2),
        name="conformer_conv",
    )(x, mod, norm_g.reshape(1, d), w_pw1.astype(BF16), b_pw1.reshape(1, -1), dw_w,
      dw_b.reshape(1, d), ln_g.reshape(1, d), ln_b.reshape(1, d), w_pw2.astype(BF16),
      b_pw2.reshape(1, d))


def _segmented_cumsum_lanes(y, seg):
    pos = lax.broadcasted_iota(jnp.int32, y.shape, 1) % seg
    step = 1
    while step < seg:
        y = y + jnp.where(pos >= step, pltpu.roll(y, step, 1), 0.0)
        step *= 2
    return y


def _mfront_kernel(x_ref, mod_ref, ng_ref, wup_ref, cw_ref, cb_ref, wq_ref, wk_ref, wv_ref,
                   wif_ref, bif_ref, q_ref, k_ref, v_ref, xc_ref, z_ref, gcol_ref, grow_ref,
                   h_scr, xm_scr, halo_scr, *, tm, inner, nc, seg, k_scale):
    n_heads = MLSTM_HEADS

    @pl.when(pl.program_id(1) == 0)
    def _():
        halo_scr[...] = jnp.zeros_like(halo_scr)

    x = x_ref[0]
    shift, scale = mod_ref[0, 0:1, :], mod_ref[0, 1:2, :]
    h_scr[...] = _norm_modulate(x, ng_ref[...], shift, scale).astype(BF16)

    kw = cw_ref.shape[0]
    for c in range(inner // nc):
        cols = slice(c * nc, (c + 1) * nc)
        xm = jnp.dot(h_scr[...], wup_ref[:, cols], preferred_element_type=F32)
        prev8 = halo_scr[:, cols]
        halo_scr[:, cols] = xm[tm - 8:tm]
        taps = tuple(cw_ref[kw - 1 - j:kw - j, cols] for j in range(kw))
        y = _causal_conv_rows(xm, prev8, taps, cb_ref[0:1, cols])
        xm_scr[:, cols] = xm.astype(BF16)
        xc_ref[0, :, cols] = (y * jax.nn.sigmoid(y)).astype(BF16)
        zcols = slice(inner + c * nc, inner + (c + 1) * nc)
        z_ref[0, :, cols] = jnp.dot(h_scr[...], wup_ref[:, zcols],
                                    preferred_element_type=F32).astype(BF16)

    gates = jnp.broadcast_to(bif_ref[...], (tm, GATE_LANES))
    for g in range(inner // BD_GROUP):
        cols = slice(g * BD_GROUP, (g + 1) * BD_GROUP)
        xc_g = xc_ref[0, :, cols]
        q = jnp.dot(xc_g, wq_ref[g], preferred_element_type=F32)
        k = jnp.dot(xc_g, wk_ref[g], preferred_element_type=F32)
        v = jnp.dot(xm_scr[:, cols], wv_ref[g], preferred_element_type=F32)
        qb, kb, vb = q.astype(BF16), k.astype(BF16), v.astype(BF16)
        gates = gates + jnp.dot(qb, wif_ref[0, cols, :], preferred_element_type=F32)
        gates = gates + jnp.dot(kb, wif_ref[1, cols, :], preferred_element_type=F32)
        gates = gates + jnp.dot(vb, wif_ref[2, cols, :], preferred_element_type=F32)
        q_ref[0, :, cols] = qb
        k_ref[0, :, cols] = (k * k_scale).astype(BF16)
        v_ref[0, :, cols] = vb

    lane = lax.broadcasted_iota(jnp.int32, gates.shape, 1)
    mixed = jnp.where(lane < n_heads, jax.nn.log_sigmoid(gates), gates)
    rows = mixed.T[0:8]
    csum = _segmented_cumsum_lanes(rows, seg)
    row_id = lax.broadcasted_iota(jnp.int32, rows.shape, 0)
    grow_ref[0] = jnp.where(row_id < n_heads, pltpu.roll(rows, n_heads, 0) - csum, csum)
    g_lanes = jnp.concatenate(
        [jnp.where(row_id < n_heads, csum, 0.0), jnp.zeros((GATE_LANES - 8, tm), F32)], axis=0).T
    gcol_ref[0] = jnp.where(lane < n_heads, g_lanes, gates)


def _blockdiag_groups(w):
    per = BD_GROUP // QKV_BLOCK
    wg = w.reshape(-1, per, QKV_BLOCK, QKV_BLOCK)
    eye = jnp.eye(per, dtype=w.dtype)
    dense = jnp.einsum("gnio,nm->gnimo", wg, eye)
    return dense.reshape(-1, BD_GROUP, BD_GROUP).astype(BF16)


def _mfront_call(x, mod, norm_g, w_up, conv_w, conv_b, wq, wk, wv, w_if, b_if):
    b, s, d = x.shape
    inner = w_up.shape[1] // 2
    n_heads = MLSTM_HEADS
    tm = TM_MFRONT
    assert s % tm == 0 and tm % CHUNK_LEN == 0 and 2 * n_heads == 8
    dh = inner // n_heads
    w_if3 = w_if.reshape(3, inner, 2 * n_heads)
    w_if3 = jnp.concatenate([w_if3[..., n_heads:], w_if3[..., :n_heads]], axis=-1)
    w_if3 = jnp.pad(w_if3, ((0, 0), (0, 0), (0, GATE_LANES - 2 * n_heads))).astype(BF16)
    b_if2 = jnp.concatenate([b_if[n_heads:], b_if[:n_heads]])
    b_if2 = jnp.pad(b_if2, (0, GATE_LANES - 2 * n_heads)).reshape(1, GATE_LANES)
    n_groups = inner // BD_GROUP

    row_spec = pl.BlockSpec((1, tm, d), lambda i, j: (i, j, 0))
    wide_spec = pl.BlockSpec((1, tm, inner), lambda i, j: (i, j, 0))
    wide_shape = jax.ShapeDtypeStruct((b, s, inner), BF16)
    return pl.pallas_call(
        functools.partial(_mfront_kernel, tm=tm, inner=inner, nc=512, seg=CHUNK_LEN,
                          k_scale=float(dh) ** -0.5),
        grid=(b, s // tm),
        in_specs=[
            row_spec,
            pl.BlockSpec((1, 6, d), lambda i, j: (i, 0, 0)),
            _resident((1, d)),
            _resident((d, 2 * inner)),
            _resident(conv_w.shape),
            _resident((1, inner)),
            _resident((n_groups, BD_GROUP, BD_GROUP)),
            _resident((n_groups, BD_GROUP, BD_GROUP)),
            _resident((n_groups, BD_GROUP, BD_GROUP)),
            _resident((3, inner, GATE_LANES)),
            _resident((1, GATE_LANES)),
        ],
        out_specs=[
            wide_spec, wide_spec, wide_spec, wide_spec, wide_spec,
            pl.BlockSpec((1, tm, GATE_LANES), lambda i, j: (i, j, 0)),
            pl.BlockSpec((1, 8, tm), lambda i, j: (i, 0, j)),
        ],
        out_shape=[
            wide_shape, wide_shape, wide_shape, wide_shape, wide_shape,
            jax.ShapeDtypeStruct((b, s, GATE_LANES), F32),
            jax.ShapeDtypeStruct((b, 8, s), F32),
        ],
        scratch_shapes=[
            pltpu.VMEM((tm, d), BF16),
            pltpu.VMEM((tm, inner), BF16),
            pltpu.VMEM((8, inner), F32),
        ],
        compiler_params=_params(2),
        name="mlstm_front",
    )(x, mod, norm_g.reshape(1, d), w_up.astype(BF16), conv_w, conv_b.reshape(1, inner),
      _blockdiag_groups(wq), _blockdiag_groups(wk), _blockdiag_groups(wv), w_if3, b_if2)


def _mchunk_kernel(q_ref, k_ref, v_ref, xc_ref, z_ref, gcol_ref, grow_ref, x_ref, mod_ref,
                   lnw_ref, skip_ref, wdn_ref, o_ref, c_scr, n_scr, m_scr, hn_scr, *, ln, dh):
    n_heads = MLSTM_HEADS

    @pl.when(pl.program_id(1) == 0)
    def _():
        c_scr[...] = jnp.zeros_like(c_scr)
        n_scr[...] = jnp.zeros_like(n_scr)
        m_scr[...] = jnp.zeros_like(m_scr)

    causal = (lax.broadcasted_iota(jnp.int32, (ln, ln), 0)
              >= lax.broadcasted_iota(jnp.int32, (ln, ln), 1))
    for h in range(n_heads):
        cols = slice(h * dh, (h + 1) * dh)
        q, k, v = q_ref[0, :, cols], k_ref[0, :, cols], v_ref[0, :, cols]
        g = gcol_ref[0, :, h:h + 1]
        i_pre = gcol_ref[0, :, n_heads + h:n_heads + h + 1]
        r = grow_ref[0, h:h + 1, :]
        m = m_scr[h, 0:1, 0:1]
        g_tot = g[ln - 1:ln, :]

        dmat = jnp.where(causal, g + r, -jnp.inf)
        inter = m + g
        m_row = jnp.maximum(inter, jnp.max(dmat, axis=-1, keepdims=True))
        w_intra = jnp.exp(dmat - m_row)
        w_inter = jnp.exp(inter - m_row)
        s = lax.dot_general(q, k, (((1,), (1,)), ((), ())), preferred_element_type=F32) * w_intra
        c_t = c_scr[h]
        num = (w_inter * jnp.dot(q, c_t.astype(BF16), preferred_element_type=F32)
               + jnp.dot(s.astype(BF16), v, preferred_element_type=F32))
        n_row = n_scr[h]
        den = (w_inter * jnp.sum(q.astype(F32) * n_row, axis=-1, keepdims=True)
               + jnp.sum(s, axis=-1, keepdims=True))
        hh = num / jnp.maximum(jnp.abs(den), jnp.exp(-m_row))
        mu = jnp.mean(hh, axis=-1, keepdims=True)
        var = jnp.mean(jnp.square(hh - mu), axis=-1, keepdims=True)
        hn_scr[:, cols] = (hh - mu) * lax.rsqrt(var + EPS)

        decay = g_tot - g + i_pre
        m_new = jnp.maximum(m + g_tot, jnp.max(decay, axis=0, keepdims=True))
        w_s = jnp.exp(decay - m_new)
        w_c = jnp.exp(m + g_tot - m_new)
        vw = (w_s * v.astype(F32)).astype(BF16)
        d_c = lax.dot_general(k, vw, (((0,), (0,)), ((), ())), preferred_element_type=F32)
        c_scr[h] = w_c * c_t + d_c
        n_scr[h] = w_c * n_row + jnp.sum(w_s * k.astype(F32), axis=0, keepdims=True)
        m_scr[h] = jnp.broadcast_to(m_new, m_scr.shape[1:])

    z = z_ref[0].astype(F32)
    pre = ((hn_scr[...] * lnw_ref[...] + skip_ref[...] * xc_ref[0].astype(F32))
           * (z * jax.nn.sigmoid(z)))
    y = jnp.dot(pre.astype(BF16), wdn_ref[...], preferred_element_type=F32)
    o_ref[0] = x_ref[0] + mod_ref[0, 2:3, :] * y


def _mchunk_call(q, k, v, xc, z, gcol, grow, x, mod, ln_w, skip, w_down):
    b, s, d = x.shape
    inner = q.shape[-1]
    n_heads = MLSTM_HEADS
    dh = inner // n_heads
    ln = CHUNK_LEN
    wide_spec = pl.BlockSpec((1, ln, inner), lambda i, j: (i, j, 0))
    row_spec = pl.BlockSpec((1, ln, d), lambda i, j: (i, j, 0))
    return pl.pallas_call(
        functools.partial(_mchunk_kernel, ln=ln, dh=dh),
        grid=(b, s // ln),
        in_specs=[
            wide_spec, wide_spec, wide_spec, wide_spec, wide_spec,
            pl.BlockSpec((1, ln, GATE_LANES), lambda i, j: (i, j, 0)),
            pl.BlockSpec((1, 8, ln), lambda i, j: (i, 0, j)),
            row_spec,
            pl.BlockSpec((1, 6, d), lambda i, j: (i, 0, 0)),
            _resident((1, inner)),
            _resident((1, inner)),
            _resident((inner, d)),
        ],
        out_specs=row_spec,
        out_shape=jax.ShapeDtypeStruct((b, s, d), F32),
        scratch_shapes=[
            pltpu.VMEM((n_heads, dh, dh), F32),
            pltpu.VMEM((n_heads, 1, dh), F32),
            pltpu.VMEM((n_heads, 8, 128), F32),
            pltpu.VMEM((ln, inner), F32),
        ],
        compiler_params=_params(2),
        name="mlstm_chunk",
    )(q, k, v, xc, z, gcol, grow, x, mod, ln_w.reshape(1, inner), skip.reshape(1, inner),
      w_down.astype(BF16))


def kernel(x, c, ada_w, ada_b, norm_mix_g, norm_ffn_g, final_g, mlstm_w_up, mlstm_conv_w,
           mlstm_conv_b, mlstm_wq, mlstm_wk, mlstm_wv, mlstm_w_if, mlstm_b_if, mlstm_ln_w,
           mlstm_skip, mlstm_w_down, conf_w_pw1, conf_b_pw1, conf_dw_w, conf_dw_b, conf_ln_g,
           conf_ln_b, conf_w_pw2, conf_b_pw2, ffn_w_up, ffn_conv_w, ffn_conv_b, ffn_w_down):
    depth = ada_w.shape[0]
    b, _, d = x.shape
    mods = _mods_call(c, ada_w, ada_b).reshape(depth, b, 6, d)
    for layer in range(depth):
        mod = mods[layer]
        j = layer // 2
        if layer % 2 == 0:
            q, k, v, xc, z, gcol, grow = _mfront_call(
                x, mod, norm_mix_g[layer], mlstm_w_up[j], mlstm_conv_w[j], mlstm_conv_b[j],
                mlstm_wq[j], mlstm_wk[j], mlstm_wv[j], mlstm_w_if[j], mlstm_b_if[j])
            x = _mchunk_call(q, k, v, xc, z, gcol, grow, x, mod, mlstm_ln_w[j], mlstm_skip[j],
                             mlstm_w_down[j])
        else:
            x = _conf_call(x, mod, norm_mix_g[layer], conf_w_pw1[j], conf_b_pw1[j], conf_dw_w[j],
                           conf_dw_b[j], conf_ln_g[j], conf_ln_b[j], conf_w_pw2[j], conf_b_pw2[j])
        x = _ffn_call(x, mod, norm_ffn_g[layer], ffn_w_up[layer], ffn_conv_w[layer],
                      ffn_conv_b[layer], ffn_w_down[layer],
                      final_g if layer == depth - 1 else None)
    return x
```

```python
import functools

import jax
import jax.numpy as jnp
import numpy as np
from jax import lax
from jax.experimental import pallas as pl
from jax.experimental.pallas import tpu as pltpu

F32 = jnp.float32
BF16 = jnp.bfloat16
EPS = 1e-6

MLSTM_HEADS = 4
QKV_BLOCK = 4
GATE_LANES = 128
GATE_GROUP = 16
BD_GROUP = 256

TM_FFN = 512
TM_CONF = 256
TM_MFRONT = 256
CHUNK_LEN = 256

VMEM_LIMIT_BYTES = 56 * 1024 * 1024


def _params(n_axes):
    return pltpu.CompilerParams(
        dimension_semantics=("arbitrary",) * n_axes,
        vmem_limit_bytes=VMEM_LIMIT_BYTES,
    )


def _resident(shape):
    zeros = (0,) * len(shape)
    return pl.BlockSpec(shape, lambda *_: zeros, pipeline_mode=pl.Buffered(1))


def _rmsnorm(x, g):
    return x * lax.rsqrt(jnp.mean(x * x, axis=-1, keepdims=True) + EPS) * g


def _norm_modulate(x, g, shift, scale):
    return _rmsnorm(x, g) * (1.0 + scale) + shift


def _causal_conv_rows(u, prev8, taps, bias):
    top = u[0:8]
    row8 = lax.broadcasted_iota(jnp.int32, top.shape, 0)
    y = taps[0] * u + bias
    y_top = taps[0] * top + bias
    for j in range(1, len(taps)):
        y = y + taps[j] * pltpu.roll(u, j, 0)
        shifted_top = jnp.where(row8 < j, pltpu.roll(prev8, j, 0), pltpu.roll(top, j, 0))
        y_top = y_top + taps[j] * shifted_top
    return jnp.concatenate([y_top, y[8:]], axis=0)


def _mods_kernel(c_ref, w_ref, b_ref, o_ref):
    c = c_ref[...]
    c_act = (c * jax.nn.sigmoid(c)).astype(BF16)
    o_ref[0] = jnp.dot(c_act, w_ref[0].astype(BF16), preferred_element_type=F32) + b_ref[0]


def _mods_call(c, ada_w, ada_b):
    depth, d, n = ada_w.shape
    b = c.shape[0]
    tn = 1024
    return pl.pallas_call(
        _mods_kernel,
        grid=(depth, n // tn),
        in_specs=[
            pl.BlockSpec((b, d), lambda l, j: (0, 0)),
            pl.BlockSpec((1, d, tn), lambda l, j: (l, 0, j)),
            pl.BlockSpec((1, 1, tn), lambda l, j: (l, 0, j)),
        ],
        out_specs=pl.BlockSpec((1, b, tn), lambda l, j: (l, 0, j)),
        out_shape=jax.ShapeDtypeStruct((depth, b, n), F32),
        compiler_params=_params(2),
        name="adaln_mods",
    )(c, ada_w, ada_b.reshape(depth, 1, n))


def _ffn_kernel(x_ref, mod_ref, ng_ref, wup_ref, cw_ref, cb_ref, wdn_ref, *rest, tm, f_dim, fc,
                final):
    if final:
        fg_ref, o_ref, h_scr, act_scr, halo_scr = rest
    else:
        o_ref, h_scr, act_scr, halo_scr = rest

    @pl.when(pl.program_id(1) == 0)
    def _():
        halo_scr[...] = jnp.zeros_like(halo_scr)

    x = x_ref[0]
    shift, scale, gate = mod_ref[0, 3:4, :], mod_ref[0, 4:5, :], mod_ref[0, 5:6, :]
    h_scr[...] = _norm_modulate(x, ng_ref[...], shift, scale).astype(BF16)

    for f in range(f_dim // fc):
        halves = []
        for half in range(2):
            c0 = half * f_dim + f * fc
            cols = slice(c0, c0 + fc)
            u = jnp.dot(h_scr[...], wup_ref[:, cols], preferred_element_type=F32)
            prev8 = halo_scr[:, cols]
            halo_scr[:, cols] = u[tm - 8:tm]
            taps = (cw_ref[2:3, cols], cw_ref[1:2, cols], cw_ref[0:1, cols])
            halves.append(_causal_conv_rows(u, prev8, taps, cb_ref[0:1, cols]))
        yg, yv = halves
        act_scr[:, f * fc:(f + 1) * fc] = (yg * jax.nn.sigmoid(yg) * yv).astype(BF16)

    y = jnp.dot(act_scr[...], wdn_ref[...], preferred_element_type=F32)
    out = x + gate * y
    if final:
        out = _rmsnorm(out, fg_ref[...])
    o_ref[0] = out


def _ffn_call(x, mod, norm_g, w_up, conv_w, conv_b, w_down, final_g=None):
    b, s, d = x.shape
    f_dim = w_down.shape[0]
    tm, fc = TM_FFN, 256
    assert s % tm == 0 and f_dim % fc == 0
    final = final_g is not None
    row_spec = pl.BlockSpec((1, tm, d), lambda i, j: (i, j, 0))
    in_specs = [
        row_spec,
        pl.BlockSpec((1, 6, d), lambda i, j: (i, 0, 0)),
        _resident((1, d)),
        _resident((d, 2 * f_dim)),
        _resident((3, 2 * f_dim)),
        _resident((1, 2 * f_dim)),
        _resident((f_dim, d)),
    ]
    args = [x, mod, norm_g.reshape(1, d), w_up.astype(BF16), conv_w, conv_b.reshape(1, -1),
            w_down.astype(BF16)]
    if final:
        in_specs.append(_resident((1, d)))
        args.append(final_g.reshape(1, d))
    return pl.pallas_call(
        functools.partial(_ffn_kernel, tm=tm, f_dim=f_dim, fc=fc, final=final),
        grid=(b, s // tm),
        in_specs=in_specs,
        out_specs=row_spec,
        out_shape=jax.ShapeDtypeStruct((b, s, d), F32),
        scratch_shapes=[
            pltpu.VMEM((tm, d), BF16),
            pltpu.VMEM((tm, f_dim), BF16),
            pltpu.VMEM((8, 2 * f_dim), F32),
        ],
        compiler_params=_params(2),
        name="conv_ffn",
    )(*args)


CONF_PAD = 32
CONF_LANE_PAD = 128


def _conf_kernel(x_ref, mod_ref, ng_ref, w1_ref, b1_ref, dw_ref, db_ref, lg_ref, lb_ref, w2_ref,
                 b2_ref, o_ref, ext_scr, shift_scr, y_scr, taps_scr, *, tm, d, kw, rb, cw):
    @pl.when(pl.program_id(1) == 0)
    def _():
        ext_scr[0:CONF_PAD, 0:d] = jnp.zeros((CONF_PAD, d), F32)

    @pl.when((pl.program_id(0) == 0) & (pl.program_id(1) == 0))
    def _():
        for k in range(kw):
            taps_scr[k] = jnp.broadcast_to(dw_ref[k:k + 1, :], (8, d))

    x = x_ref[0]
    shift, scale, gate = mod_ref[0, 0:1, :], mod_ref[0, 1:2, :], mod_ref[0, 2:3, :]
    h = _norm_modulate(x, ng_ref[...], shift, scale).astype(BF16)
    a = jnp.dot(h, w1_ref[:, 0:d], preferred_element_type=F32) + b1_ref[0:1, 0:d]
    g = jnp.dot(h, w1_ref[:, d:2 * d], preferred_element_type=F32) + b1_ref[0:1, d:2 * d]
    ext_scr[CONF_PAD:CONF_PAD + tm, 0:d] = a * jax.nn.sigmoid(g)

    span = tm + CONF_PAD - 8
    ext = ext_scr[:, 0:d]
    for b in range(1, 8):
        shift_scr[b - 1, :, 0:d] = pltpu.roll(ext, tm + CONF_PAD - b, 0)[0:span]

    base = CONF_PAD - (kw - 1)

    for c in range(d // cw):
        cols = slice(c * cw, (c + 1) * cw)
        taps = [jnp.concatenate([taps_scr[k, :, cols]] * (rb // 8), axis=0) for k in range(kw)]
        bias = jnp.broadcast_to(db_ref[0:1, cols], (rb, cw))

        def row_block(i, carry, cols=cols, taps=taps, bias=bias):
            r0 = pl.multiple_of(i * rb, rb)
            acc = bias
            for k in range(kw):
                q8, b = divmod(base + k, 8)
                if b == 0:
                    win = ext_scr[pl.ds(r0 + 8 * q8, rb), cols]
                else:
                    win = shift_scr[b - 1, pl.ds(r0 + 8 * q8, rb), cols]
                acc = acc + taps[k] * win
            y_scr[pl.ds(r0, rb), cols] = acc
            return carry

        lax.fori_loop(0, tm // rb, row_block, 0)
    ext_scr[0:CONF_PAD, 0:d] = ext_scr[tm:tm + CONF_PAD, 0:d]

    y = y_scr[...]
    mu = jnp.mean(y, axis=-1, keepdims=True)
    var = jnp.mean(jnp.square(y - mu), axis=-1, keepdims=True)
    ln = (y - mu) * lax.rsqrt(var + EPS) * lg_ref[...] + lb_ref[...]
    act = (ln * jax.nn.sigmoid(ln)).astype(BF16)
    out = jnp.dot(act, w2_ref[...], preferred_element_type=F32) + b2_ref[...]
    o_ref[0] = x + gate * out


def _conf_call(x, mod, norm_g, w_pw1, b_pw1, dw_w, dw_b, ln_g, ln_b, w_pw2, b_pw2):
    b, s, d = x.shape
    kw = dw_w.shape[0]
    tm = TM_CONF
    assert s % tm == 0 and kw - 1 <= CONF_PAD
    row_spec = pl.BlockSpec((1, tm, d), lambda i, j: (i, j, 0))
    return pl.pallas_call(
        functools.partial(_conf_kernel, tm=tm, d=d, kw=kw, rb=64, cw=128),
        grid=(b, s // tm),
        in_specs=[
            row_spec,
            pl.BlockSpec((1, 6, d), lambda i, j: (i, 0, 0)),
            _resident((1, d)),
            _resident((d, 2 * d)),
            _resident((1, 2 * d)),
            _resident((kw, d)),
            _resident((1, d)),
            _resident((1, d)),
            _resident((1, d)),
            _resident((d, d)),
            _resident((1, d)),
        ],
        out_specs=row_spec,
        out_shape=jax.ShapeDtypeStruct((b, s, d), F32),
        scratch_shapes=[
            pltpu.VMEM((tm + CONF_PAD, d + CONF_LANE_PAD), F32),
            pltpu.VMEM((7, tm + CONF_PAD - 8, d + CONF_LANE_PAD), F32),
            pltpu.VMEM((tm, d), F32),
            pltpu.VMEM((kw, 8, d), F32),
        ],
        compiler_params=_params(2),
        name="conformer_conv",
    )(x, mod, norm_g.reshape(1, d), w_pw1.astype(BF16), b_pw1.reshape(1, -1), dw_w,
      dw_b.reshape(1, d), ln_g.reshape(1, d), ln_b.reshape(1, d), w_pw2.astype(BF16),
      b_pw2.reshape(1, d))


def _split3(x):
    hi = x.astype(BF16)
    rest = x - hi.astype(F32)
    mid = rest.astype(BF16)
    lo = (rest - mid.astype(F32)).astype(BF16)
    return hi, mid, lo


def _gate_factor_constants(tm, seg):
    gl, nh, n = GATE_GROUP, MLSTM_HEADS, GATE_LANES
    col_place = np.zeros((3, n, n), np.float32)
    row_place = np.zeros((3, n, n), np.float32)
    col_ones = np.zeros((1, n), np.float32)
    row_ones = np.zeros((n, n), np.float32)
    expand = np.zeros((n, 2 * n), np.float32)
    for h in range(nh):
        for j in range(3):
            col_place[j, h, gl * h + j] = 1.0
            col_place[j, nh + h, gl * h + 9 + j] = 1.0
            row_place[j, gl * h + 3 + j, nh + h] = 1.0
            row_place[j, gl * h + 6 + j, h] = -1.0
            col_ones[0, gl * h + 3 + j] = 1.0
            col_ones[0, gl * h + 6 + j] = 1.0
            row_ones[gl * h + j, :] = 1.0
            expand[gl * h + j, :n] = 1.0
            expand[gl * h + j, n:] = -1.0
            expand[gl * h + 9 + j, n:] = 1.0
    pos = np.arange(tm)
    tri = ((pos[:, None] >= pos[None, :]) & (pos[:, None] // seg == pos[None, :] // seg))
    as_bf16 = lambda a: jnp.asarray(a, BF16)
    return (as_bf16(col_place), as_bf16(row_place), jnp.asarray(col_ones), jnp.asarray(row_ones),
            as_bf16(expand), as_bf16(tri.astype(np.float32)))


def _mfront_kernel(x_ref, mod_ref, ng_ref, wup_ref, cw_ref, cb_ref, wq_ref, wk_ref, wkt_ref, wv_ref,
                   wif_ref, bif_ref, cplace_ref, rplace_ref, cones_ref, rones_ref, tri_ref,
                   q_ref, k_ref, kt_ref, v_ref, xc_ref, z_ref, gcol_ref, grow_ref,
                   h_scr, halo_scr, *, tm, inner, nc, k_scale):
    n_heads = MLSTM_HEADS

    @pl.when(pl.program_id(1) == 0)
    def _():
        halo_scr[...] = jnp.zeros_like(halo_scr)

    x = x_ref[0]
    shift, scale = mod_ref[0, 0:1, :], mod_ref[0, 1:2, :]
    h_scr[...] = _norm_modulate(x, ng_ref[...], shift, scale).astype(BF16)

    kw = cw_ref.shape[0]
    gates = jnp.broadcast_to(bif_ref[...], (tm, GATE_LANES))
    n_chunks = inner // nc
    xm_next = jnp.dot(h_scr[...], wup_ref[:, 0:nc], preferred_element_type=F32)
    for c in range(n_chunks):
        cols = slice(c * nc, (c + 1) * nc)
        xm = xm_next
        if c + 1 < n_chunks:
            xm_next = jnp.dot(h_scr[...], wup_ref[:, (c + 1) * nc:(c + 2) * nc],
                              preferred_element_type=F32)
        prev8 = halo_scr[:, cols]
        halo_scr[:, cols] = xm[tm - 8:tm]
        taps = tuple(cw_ref[kw - 1 - j:kw - j, cols] for j in range(kw))
        y = _causal_conv_rows(xm, prev8, taps, cb_ref[0:1, cols])
        xm_b = xm.astype(BF16)
        xc_b = (y * jax.nn.sigmoid(y)).astype(BF16)
        xc_ref[0, :, cols] = xc_b

        for g in range(c * nc // BD_GROUP, (c + 1) * nc // BD_GROUP):
            gcols = slice(g * BD_GROUP, (g + 1) * BD_GROUP)
            sub = slice(g * BD_GROUP - c * nc, (g + 1) * BD_GROUP - c * nc)
            xc_g = xc_b[:, sub]
            q = jnp.dot(xc_g, wq_ref[g], preferred_element_type=F32)
            k = jnp.dot(xc_g, wk_ref[g], preferred_element_type=F32)
            v = jnp.dot(xm_b[:, sub], wv_ref[g], preferred_element_type=F32)
            qb, kb, vb = q.astype(BF16), k.astype(BF16), v.astype(BF16)
            gates = gates + jnp.dot(qb, wif_ref[0, gcols, :], preferred_element_type=F32)
            gates = gates + jnp.dot(kb, wif_ref[1, gcols, :], preferred_element_type=F32)
            gates = gates + jnp.dot(vb, wif_ref[2, gcols, :], preferred_element_type=F32)
            q_ref[0, :, gcols] = qb
            k_ref[0, :, gcols] = (k * k_scale).astype(BF16)
            k_t = lax.dot_general(wkt_ref[g], xc_g, (((1,), (1,)), ((), ())),
                                  preferred_element_type=F32)
            kt_ref[0, gcols, :] = (k_t * k_scale).astype(BF16)
            v_ref[0, :, gcols] = vb

        zcols = slice(inner + c * nc, inner + (c + 1) * nc)
        z_ref[0, :, cols] = jnp.dot(h_scr[...], wup_ref[:, zcols],
                                    preferred_element_type=F32).astype(BF16)

    lane = lax.broadcasted_iota(jnp.int32, gates.shape, 1)
    log_f = jax.nn.log_sigmoid(gates)
    csum = jnp.zeros((tm, GATE_LANES), F32)
    for piece in _split3(log_f):
        csum = csum + jnp.dot(tri_ref[...], piece, preferred_element_type=F32)
    gi = jnp.where(lane < n_heads, csum, gates)
    gcol = jnp.broadcast_to(cones_ref[...], (tm, GATE_LANES))
    grow = jnp.concatenate([rones_ref[...]] * (tm // GATE_LANES), axis=1)
    for j, piece in enumerate(_split3(gi)):
        gcol = gcol + jnp.dot(piece, cplace_ref[j], preferred_element_type=F32)
        grow = grow + lax.dot_general(rplace_ref[j], piece, (((1,), (1,)), ((), ())),
                                      preferred_element_type=F32)
    gcol_ref[0] = gcol.astype(BF16)
    grow_ref[0] = grow.astype(BF16)


def _blockdiag_groups(w):
    per = BD_GROUP // QKV_BLOCK
    wg = w.reshape(-1, per, QKV_BLOCK, QKV_BLOCK)
    eye = jnp.eye(per, dtype=w.dtype)
    dense = jnp.einsum("gnio,nm->gnimo", wg, eye)
    return dense.reshape(-1, BD_GROUP, BD_GROUP).astype(BF16)


def _mfront_call(x, mod, norm_g, w_up, conv_w, conv_b, wq, wk, wv, w_if, b_if):
    b, s, d = x.shape
    inner = w_up.shape[1] // 2
    n_heads = MLSTM_HEADS
    tm = TM_MFRONT
    assert s % tm == 0 and tm % CHUNK_LEN == 0 and 2 * n_heads == 8
    dh = inner // n_heads
    w_if3 = w_if.reshape(3, inner, 2 * n_heads)
    w_if3 = jnp.concatenate([w_if3[..., n_heads:], w_if3[..., :n_heads]], axis=-1)
    w_if3 = jnp.pad(w_if3, ((0, 0), (0, 0), (0, GATE_LANES - 2 * n_heads))).astype(BF16)
    b_if2 = jnp.concatenate([b_if[n_heads:], b_if[:n_heads]])
    b_if2 = jnp.pad(b_if2, (0, GATE_LANES - 2 * n_heads)).reshape(1, GATE_LANES)
    n_groups = inner // BD_GROUP
    wk_groups = _blockdiag_groups(wk)
    col_place, row_place, col_ones, row_ones, _, tri = _gate_factor_constants(tm, CHUNK_LEN)

    row_spec = pl.BlockSpec((1, tm, d), lambda i, j: (i, j, 0))
    wide_spec = pl.BlockSpec((1, tm, inner), lambda i, j: (i, j, 0))
    wide_shape = jax.ShapeDtypeStruct((b, s, inner), BF16)
    return pl.pallas_call(
        functools.partial(_mfront_kernel, tm=tm, inner=inner, nc=512, k_scale=float(dh) ** -0.5),
        grid=(b, s // tm),
        in_specs=[
            row_spec,
            pl.BlockSpec((1, 6, d), lambda i, j: (i, 0, 0)),
            _resident((1, d)),
            _resident((d, 2 * inner)),
            _resident(conv_w.shape),
            _resident((1, inner)),
            _resident((n_groups, BD_GROUP, BD_GROUP)),
            _resident((n_groups, BD_GROUP, BD_GROUP)),
            _resident((n_groups, BD_GROUP, BD_GROUP)),
            _resident((n_groups, BD_GROUP, BD_GROUP)),
            _resident((3, inner, GATE_LANES)),
            _resident((1, GATE_LANES)),
            _resident((3, GATE_LANES, GATE_LANES)),
            _resident((3, GATE_LANES, GATE_LANES)),
            _resident((1, GATE_LANES)),
            _resident((GATE_LANES, GATE_LANES)),
            _resident((tm, tm)),
        ],
        out_specs=[
            wide_spec, wide_spec,
            pl.BlockSpec((1, inner, tm), lambda i, j: (i, 0, j)),
            wide_spec, wide_spec, wide_spec,
            pl.BlockSpec((1, tm, GATE_LANES), lambda i, j: (i, j, 0)),
            pl.BlockSpec((1, GATE_LANES, tm), lambda i, j: (i, 0, j)),
        ],
        out_shape=[
            wide_shape, wide_shape,
            jax.ShapeDtypeStruct((b, inner, s), BF16),
            wide_shape, wide_shape, wide_shape,
            jax.ShapeDtypeStruct((b, s, GATE_LANES), BF16),
            jax.ShapeDtypeStruct((b, GATE_LANES, s), BF16),
        ],
        scratch_shapes=[
            pltpu.VMEM((tm, d), BF16),
            pltpu.VMEM((8, inner), F32),
        ],
        compiler_params=_params(2),
        name="mlstm_front",
    )(x, mod, norm_g.reshape(1, d), w_up.astype(BF16), conv_w, conv_b.reshape(1, inner),
      _blockdiag_groups(wq), wk_groups, jnp.swapaxes(wk_groups, 1, 2), _blockdiag_groups(wv),
      w_if3, b_if2, col_place, row_place, col_ones, row_ones, tri)


def _lanes(a, n):
    return jnp.concatenate([a] * n, axis=1)


def _mchunk_kernel(q_ref, k_ref, kt_ref, v_ref, xc_ref, z_ref, gcol_ref, grow_ref, expand_ref,
                   x_ref, mod_ref, lnw_ref, skip_ref, wdn_ref, o_ref, c_scr, m_scr, hn_scr, *,
                   ln, dh):
    n_heads = MLSTM_HEADS
    rep = GATE_LANES
    n_rep = dh // rep

    @pl.when(pl.program_id(1) == 0)
    def _():
        c_scr[...] = jnp.zeros_like(c_scr)
        m_scr[...] = jnp.zeros_like(m_scr)

    causal = (lax.broadcasted_iota(jnp.int32, (ln, ln), 0)
              >= lax.broadcasted_iota(jnp.int32, (ln, ln), 1))
    group = lax.broadcasted_iota(jnp.int32, (ln, rep), 1) // GATE_GROUP
    gcol = gcol_ref[0]
    for h in range(n_heads):
        cols = slice(h * dh, (h + 1) * dh)
        q, k, v = q_ref[0, :, cols], k_ref[0, :, cols], v_ref[0, :, cols]
        gcol_h = jnp.where(group == h, gcol, jnp.zeros_like(gcol))
        dmat = jnp.dot(gcol_h, grow_ref[0], preferred_element_type=F32)
        gi = jnp.dot(gcol_h, expand_ref[...], preferred_element_type=F32)
        g, i_minus_g = gi[:, 0:rep], gi[:, rep:2 * rep]
        m = m_scr[h, 0:1, :]
        g_tot = g[ln - 1:ln, :]

        dmat = jnp.where(causal, dmat, -jnp.inf)
        inter = m + g
        m_row = jnp.maximum(inter, jnp.max(dmat, axis=-1, keepdims=True))
        w_intra = jnp.exp(dmat - _lanes(m_row, ln // rep))
        w_inter = jnp.exp(inter - m_row)
        s = lax.dot_general(q, k, (((1,), (1,)), ((), ())), preferred_element_type=F32) * w_intra
        c_t = c_scr[h]
        qc = jnp.dot(q, c_t.astype(BF16), preferred_element_type=F32)
        num = (_lanes(w_inter, n_rep) * qc[:, 0:dh]
               + jnp.dot(s.astype(BF16), v, preferred_element_type=F32))
        den = w_inter * qc[:, dh:dh + rep] + jnp.sum(s, axis=-1, keepdims=True)
        inv = 1.0 / jnp.maximum(jnp.abs(den), jnp.exp(-m_row))
        hh = num * _lanes(inv, n_rep)
        mu = jnp.mean(hh, axis=-1, keepdims=True)
        var = jnp.mean(jnp.square(hh - mu), axis=-1, keepdims=True)
        hn_scr[:, cols] = (hh - mu) * lax.rsqrt(var + EPS)

        decay = i_minus_g + g_tot
        m_new = jnp.maximum(m + g_tot, jnp.max(decay, axis=0, keepdims=True))
        w_s = jnp.exp(decay - m_new)
        w_c = jnp.exp(m + g_tot - m_new)
        vw = jnp.concatenate(
            [(_lanes(w_s, n_rep) * v.astype(F32)).astype(BF16), w_s.astype(BF16)], axis=1)
        d_c = jnp.dot(kt_ref[0, cols, :], vw, preferred_element_type=F32)
        c_scr[h] = _lanes(w_c, n_rep + 1) * c_t + d_c
        m_scr[h] = jnp.broadcast_to(m_new, m_scr.shape[1:])

    z = z_ref[0].astype(F32)
    pre = ((hn_scr[...] * lnw_ref[...] + skip_ref[...] * xc_ref[0].astype(F32))
           * (z * jax.nn.sigmoid(z)))
    y = jnp.dot(pre.astype(BF16), wdn_ref[...], preferred_element_type=F32)
    o_ref[0] = x_ref[0] + mod_ref[0, 2:3, :] * y


def _mchunk_call(q, k, k_t, v, xc, z, gcol, grow, x, mod, ln_w, skip, w_down):
    b, s, d = x.shape
    inner = q.shape[-1]
    n_heads = MLSTM_HEADS
    dh = inner // n_heads
    ln = CHUNK_LEN
    wide_spec = pl.BlockSpec((1, ln, inner), lambda i, j: (i, j, 0))
    row_spec = pl.BlockSpec((1, ln, d), lambda i, j: (i, j, 0))
    return pl.pallas_call(
        functools.partial(_mchunk_kernel, ln=ln, dh=dh),
        grid=(b, s // ln),
        in_specs=[
            wide_spec, wide_spec,
            pl.BlockSpec((1, inner, ln), lambda i, j: (i, 0, j)),
            wide_spec, wide_spec, wide_spec,
            pl.BlockSpec((1, ln, GATE_LANES), lambda i, j: (i, j, 0)),
            pl.BlockSpec((1, GATE_LANES, ln), lambda i, j: (i, 0, j)),
            _resident((GATE_LANES, 2 * GATE_LANES)),
            row_spec,
            pl.BlockSpec((1, 6, d), lambda i, j: (i, 0, 0)),
            _resident((1, inner)),
            _resident((1, inner)),
            _resident((inner, d)),
        ],
        out_specs=row_spec,
        out_shape=jax.ShapeDtypeStruct((b, s, d), F32),
        scratch_shapes=[
            pltpu.VMEM((n_heads, dh, dh + GATE_LANES), F32),
            pltpu.VMEM((n_heads, 8, GATE_LANES), F32),
            pltpu.VMEM((ln, inner), F32),
        ],
        compiler_params=_params(2),
        name="mlstm_chunk",
    )(q, k, k_t, v, xc, z, gcol, grow, _gate_factor_constants(ln, ln)[4], x, mod,
      ln_w.reshape(1, inner), skip.reshape(1, inner), w_down.astype(BF16))


def kernel(x, c, ada_w, ada_b, norm_mix_g, norm_ffn_g, final_g, mlstm_w_up, mlstm_conv_w,
           mlstm_conv_b, mlstm_wq, mlstm_wk, mlstm_wv, mlstm_w_if, mlstm_b_if, mlstm_ln_w,
           mlstm_skip, mlstm_w_down, conf_w_pw1, conf_b_pw1, conf_dw_w, conf_dw_b, conf_ln_g,
           conf_ln_b, conf_w_pw2, conf_b_pw2, ffn_w_up, ffn_conv_w, ffn_conv_b, ffn_w_down):
    depth = ada_w.shape[0]
    b, _, d = x.shape
    mods = _mods_call(c, ada_w, ada_b).reshape(depth, b, 6, d)
    for layer in range(depth):
        mod = mods[layer]
        j = layer // 2
        if layer % 2 == 0:
            q, k, k_t, v, xc, z, gcol, grow = _mfront_call(
                x, mod, norm_mix_g[layer], mlstm_w_up[j], mlstm_conv_w[j], mlstm_conv_b[j],
                mlstm_wq[j], mlstm_wk[j], mlstm_wv[j], mlstm_w_if[j], mlstm_b_if[j])
            x = _mchunk_call(q, k, k_t, v, xc, z, gcol, grow, x, mod, mlstm_ln_w[j], mlstm_skip[j],
                             mlstm_w_down[j])
        else:
            x = _conf_call(x, mod, norm_mix_g[layer], conf_w_pw1[j], conf_b_pw1[j], conf_dw_w[j],
                           conf_dw_b[j], conf_ln_g[j], conf_ln_b[j], conf_w_pw2[j], conf_b_pw2[j])
        x = _ffn_call(x, mod, norm_ffn_g[layer], ffn_w_up[layer], ffn_conv_w[layer],
                      ffn_conv_b[layer], ffn_w_down[layer],
                      final_g if layer == depth - 1 else None)
    return x
```

```python
import functools

import jax
import jax.numpy as jnp
import numpy as np
from jax import lax
from jax.experimental import pallas as pl
from jax.experimental.pallas import tpu as pltpu

F32 = jnp.float32
BF16 = jnp.bfloat16
EPS = 1e-6

MLSTM_HEADS = 4
QKV_BLOCK = 4
GATE_LANES = 128
GATE_GROUP = 16
BD_GROUP = 256

TM_FFN = 512
TM_CONF = 256
TM_MFRONT = 256
CHUNK_LEN = 256

VMEM_LIMIT_BYTES = 56 * 1024 * 1024


def _params(n_axes):
    return pltpu.CompilerParams(
        dimension_semantics=("arbitrary",) * n_axes,
        vmem_limit_bytes=VMEM_LIMIT_BYTES,
    )


def _resident(shape):
    zeros = (0,) * len(shape)
    return pl.BlockSpec(shape, lambda *_: zeros, pipeline_mode=pl.Buffered(1))


def _resident_layer(stacked, layer):
    shape = stacked.shape[1:]
    index = (layer,) + (0,) * len(shape)
    return pl.BlockSpec((None,) + shape, lambda *_: index, pipeline_mode=pl.Buffered(1))


def _rmsnorm(x, g):
    return x * lax.rsqrt(jnp.mean(x * x, axis=-1, keepdims=True) + EPS) * g


def _norm_modulate(x, g, shift, scale):
    return _rmsnorm(x, g) * (1.0 + scale) + shift


def _causal_conv_rows(u, prev8, taps, bias):
    top = u[0:8]
    row8 = lax.broadcasted_iota(jnp.int32, top.shape, 0)
    y = taps[0] * u + bias
    y_top = taps[0] * top + bias
    for j in range(1, len(taps)):
        y = y + taps[j] * pltpu.roll(u, j, 0)
        shifted_top = jnp.where(row8 < j, pltpu.roll(prev8, j, 0), pltpu.roll(top, j, 0))
        y_top = y_top + taps[j] * shifted_top
    return jnp.concatenate([y_top, y[8:]], axis=0)


def _mods_kernel(c_ref, w_ref, b_ref, o_ref):
    c = c_ref[...]
    c_act = (c * jax.nn.sigmoid(c)).astype(BF16)
    o_ref[0] = jnp.dot(c_act, w_ref[0].astype(BF16), preferred_element_type=F32) + b_ref[0]


def _mods_call(c, ada_w, ada_b):
    depth, d, n = ada_w.shape
    b = c.shape[0]
    tn = 1024
    return pl.pallas_call(
        _mods_kernel,
        grid=(depth, n // tn),
        in_specs=[
            pl.BlockSpec((b, d), lambda l, j: (0, 0)),
            pl.BlockSpec((1, d, tn), lambda l, j: (l, 0, j)),
            pl.BlockSpec((1, 1, tn), lambda l, j: (l, 0, j)),
        ],
        out_specs=pl.BlockSpec((1, b, tn), lambda l, j: (l, 0, j)),
        out_shape=jax.ShapeDtypeStruct((depth, b, n), F32),
        compiler_params=_params(2),
        name="adaln_mods",
    )(c, ada_w, ada_b.reshape(depth, 1, n))


def _ffn_kernel(x_ref, mod_ref, ng_ref, wup_ref, cw_ref, cb_ref, wdn_ref, *rest, tm, f_dim, fc,
                final):
    if final:
        fg_ref, o_ref, h_scr, act_scr, halo_scr = rest
    else:
        o_ref, h_scr, act_scr, halo_scr = rest

    @pl.when(pl.program_id(1) == 0)
    def _():
        halo_scr[...] = jnp.zeros_like(halo_scr)

    x = x_ref[0]
    shift, scale, gate = mod_ref[0, 3:4, :], mod_ref[0, 4:5, :], mod_ref[0, 5:6, :]
    h_scr[...] = _norm_modulate(x, ng_ref[...], shift, scale).astype(BF16)

    for f in range(f_dim // fc):
        halves = []
        for half in range(2):
            c0 = half * f_dim + f * fc
            cols = slice(c0, c0 + fc)
            u = jnp.dot(h_scr[...], wup_ref[:, cols], preferred_element_type=F32)
            prev8 = halo_scr[:, cols]
            halo_scr[:, cols] = u[tm - 8:tm]
            taps = (cw_ref[2:3, cols], cw_ref[1:2, cols], cw_ref[0:1, cols])
            halves.append(_causal_conv_rows(u, prev8, taps, cb_ref[0:1, cols]))
        yg, yv = halves
        act_scr[:, f * fc:(f + 1) * fc] = (yg * jax.nn.sigmoid(yg) * yv).astype(BF16)

    y = jnp.dot(act_scr[...], wdn_ref[...], preferred_element_type=F32)
    out = x + gate * y
    if final:
        out = _rmsnorm(out, fg_ref[...])
    o_ref[0] = out


def _ffn_call(x, mod, norm_g, w_up_all, conv_w, conv_b, w_down_all, layer, final_g=None,
              tm=TM_FFN, fc=256):
    b, s, d = x.shape
    f_dim = w_down_all.shape[1]
    assert s % tm == 0 and f_dim % fc == 0
    final = final_g is not None
    row_spec = pl.BlockSpec((1, tm, d), lambda i, j: (i, j, 0))
    in_specs = [
        row_spec,
        pl.BlockSpec((1, 6, d), lambda i, j: (i, 0, 0)),
        _resident((1, d)),
        _resident_layer(w_up_all, layer),
        _resident((3, 2 * f_dim)),
        _resident((1, 2 * f_dim)),
        _resident_layer(w_down_all, layer),
    ]
    args = [x, mod, norm_g.reshape(1, d), w_up_all, conv_w, conv_b.reshape(1, -1), w_down_all]
    if final:
        in_specs.append(_resident((1, d)))
        args.append(final_g.reshape(1, d))
    return pl.pallas_call(
        functools.partial(_ffn_kernel, tm=tm, f_dim=f_dim, fc=fc, final=final),
        grid=(b, s // tm),
        in_specs=in_specs,
        out_specs=row_spec,
        out_shape=jax.ShapeDtypeStruct((b, s, d), F32),
        scratch_shapes=[
            pltpu.VMEM((tm, d), BF16),
            pltpu.VMEM((tm, f_dim), BF16),
            pltpu.VMEM((8, 2 * f_dim), F32),
        ],
        compiler_params=_params(2),
        name="conv_ffn",
    )(*args)


CONF_PAD = 32
CONF_LANE_PAD = 128


def _conf_kernel(x_ref, mod_ref, ng_ref, w1_ref, b1_ref, dw_ref, db_ref, lg_ref, lb_ref, w2_ref,
                 b2_ref, o_ref, ext_scr, shift_scr, y_scr, taps_scr, *, tm, d, kw, rb, cw,
                 hoist_taps):
    @pl.when(pl.program_id(1) == 0)
    def _():
        ext_scr[0:CONF_PAD, 0:d] = jnp.zeros((CONF_PAD, d), F32)

    @pl.when((pl.program_id(0) == 0) & (pl.program_id(1) == 0))
    def _():
        for k in range(kw):
            taps_scr[k] = jnp.broadcast_to(dw_ref[k:k + 1, :], (8, d))

    x = x_ref[0]
    shift, scale, gate = mod_ref[0, 0:1, :], mod_ref[0, 1:2, :], mod_ref[0, 2:3, :]
    h = _norm_modulate(x, ng_ref[...], shift, scale).astype(BF16)
    a = jnp.dot(h, w1_ref[:, 0:d], preferred_element_type=F32) + b1_ref[0:1, 0:d]
    g = jnp.dot(h, w1_ref[:, d:2 * d], preferred_element_type=F32) + b1_ref[0:1, d:2 * d]
    ext_scr[CONF_PAD:CONF_PAD + tm, 0:d] = a * jax.nn.sigmoid(g)

    span = tm + CONF_PAD - 8
    ext = ext_scr[:, 0:d]
    for b in range(1, 8):
        shift_scr[b - 1, :, 0:d] = pltpu.roll(ext, tm + CONF_PAD - b, 0)[0:span]

    base = CONF_PAD - (kw - 1)

    def tap(k, cols):
        return jnp.concatenate([taps_scr[k, :, cols]] * (rb // 8), axis=0)

    for c in range(d // cw):
        cols = slice(c * cw, (c + 1) * cw)
        taps = [tap(k, cols) for k in range(kw)] if hoist_taps else None
        bias = jnp.broadcast_to(db_ref[0:1, cols], (rb, cw))

        def row_block(i, carry, cols=cols, taps=taps, bias=bias):
            r0 = pl.multiple_of(i * rb, rb)
            acc = bias
            for k in range(kw):
                q8, b = divmod(base + k, 8)
                if b == 0:
                    win = ext_scr[pl.ds(r0 + 8 * q8, rb), cols]
                else:
                    win = shift_scr[b - 1, pl.ds(r0 + 8 * q8, rb), cols]
                acc = acc + (taps[k] if hoist_taps else tap(k, cols)) * win
            y_scr[pl.ds(r0, rb), cols] = acc
            return carry

        lax.fori_loop(0, tm // rb, row_block, 0)
    ext_scr[0:CONF_PAD, 0:d] = ext_scr[tm:tm + CONF_PAD, 0:d]

    y = y_scr[...]
    mu = jnp.mean(y, axis=-1, keepdims=True)
    var = jnp.mean(jnp.square(y - mu), axis=-1, keepdims=True)
    ln = (y - mu) * lax.rsqrt(var + EPS) * lg_ref[...] + lb_ref[...]
    act = (ln * jax.nn.sigmoid(ln)).astype(BF16)
    out = jnp.dot(act, w2_ref[...], preferred_element_type=F32) + b2_ref[...]
    o_ref[0] = x + gate * out


def _conf_call(x, mod, norm_g, w_pw1_all, b_pw1, dw_w, dw_b, ln_g, ln_b, w_pw2_all, b_pw2, layer,
               tm=TM_CONF, rb=64, cw=128, lane_pad=CONF_LANE_PAD):
    b, s, d = x.shape
    kw = dw_w.shape[0]
    assert s % tm == 0 and kw - 1 <= CONF_PAD
    row_spec = pl.BlockSpec((1, tm, d), lambda i, j: (i, j, 0))
    return pl.pallas_call(
        functools.partial(_conf_kernel, tm=tm, d=d, kw=kw, rb=rb, cw=cw, hoist_taps=cw <= 128),
        grid=(b, s // tm),
        in_specs=[
            row_spec,
            pl.BlockSpec((1, 6, d), lambda i, j: (i, 0, 0)),
            _resident((1, d)),
            _resident_layer(w_pw1_all, layer),
            _resident((1, 2 * d)),
            _resident((kw, d)),
            _resident((1, d)),
            _resident((1, d)),
            _resident((1, d)),
            _resident_layer(w_pw2_all, layer),
            _resident((1, d)),
        ],
        out_specs=row_spec,
        out_shape=jax.ShapeDtypeStruct((b, s, d), F32),
        scratch_shapes=[
            pltpu.VMEM((tm + CONF_PAD, d + lane_pad), F32),
            pltpu.VMEM((7, tm + CONF_PAD - 8, d + lane_pad), F32),
            pltpu.VMEM((tm, d), F32),
            pltpu.VMEM((kw, 8, d), F32),
        ],
        compiler_params=_params(2),
        name="conformer_conv",
    )(x, mod, norm_g.reshape(1, d), w_pw1_all, b_pw1.reshape(1, -1), dw_w, dw_b.reshape(1, d),
      ln_g.reshape(1, d), ln_b.reshape(1, d), w_pw2_all, b_pw2.reshape(1, d))


def _split3(x):
    hi = x.astype(BF16)
    rest = x - hi.astype(F32)
    mid = rest.astype(BF16)
    lo = (rest - mid.astype(F32)).astype(BF16)
    return hi, mid, lo


def _gate_factor_constants(tm, seg):
    gl, nh, n = GATE_GROUP, MLSTM_HEADS, GATE_LANES
    col_place = np.zeros((3, n, n), np.float32)
    row_place = np.zeros((3, n, n), np.float32)
    col_ones = np.zeros((1, n), np.float32)
    row_ones = np.zeros((n, n), np.float32)
    expand = np.zeros((n, 2 * n), np.float32)
    for h in range(nh):
        for j in range(3):
            col_place[j, h, gl * h + j] = 1.0
            col_place[j, nh + h, gl * h + 9 + j] = 1.0
            row_place[j, gl * h + 3 + j, nh + h] = 1.0
            row_place[j, gl * h + 6 + j, h] = -1.0
            col_ones[0, gl * h + 3 + j] = 1.0
            col_ones[0, gl * h + 6 + j] = 1.0
            row_ones[gl * h + j, :] = 1.0
            expand[gl * h + j, :n] = 1.0
            expand[gl * h + j, n:] = -1.0
            expand[gl * h + 9 + j, n:] = 1.0
    pos = np.arange(tm)
    tri = ((pos[:, None] >= pos[None, :]) & (pos[:, None] // seg == pos[None, :] // seg))
    as_bf16 = lambda a: jnp.asarray(a, BF16)
    return (as_bf16(col_place), as_bf16(row_place), jnp.asarray(col_ones), jnp.asarray(row_ones),
            as_bf16(expand), as_bf16(tri.astype(np.float32)))


def _mfront_kernel(x_ref, mod_ref, ng_ref, wup_ref, cw_ref, cb_ref, wq_ref, wk_ref, wkt_ref, wv_ref,
                   wif_ref, bif_ref, cplace_ref, rplace_ref, cones_ref, rones_ref, tri_ref,
                   q_ref, k_ref, kt_ref, v_ref, xc_ref, z_ref, gcol_ref, grow_ref,
                   h_scr, halo_scr, *, tm, inner, nc, k_scale):
    n_heads = MLSTM_HEADS

    @pl.when(pl.program_id(1) == 0)
    def _():
        halo_scr[...] = jnp.zeros_like(halo_scr)

    x = x_ref[0]
    shift, scale = mod_ref[0, 0:1, :], mod_ref[0, 1:2, :]
    h_scr[...] = _norm_modulate(x, ng_ref[...], shift, scale).astype(BF16)

    kw = cw_ref.shape[0]
    gates = jnp.broadcast_to(bif_ref[...], (tm, GATE_LANES))
    n_chunks = inner // nc
    xm_next = jnp.dot(h_scr[...], wup_ref[:, 0:nc], preferred_element_type=F32)
    for c in range(n_chunks):
        cols = slice(c * nc, (c + 1) * nc)
        xm = xm_next
        if c + 1 < n_chunks:
            xm_next = jnp.dot(h_scr[...], wup_ref[:, (c + 1) * nc:(c + 2) * nc],
                              preferred_element_type=F32)
        prev8 = halo_scr[:, cols]
        halo_scr[:, cols] = xm[tm - 8:tm]
        taps = tuple(cw_ref[kw - 1 - j:kw - j, cols] for j in range(kw))
        y = _causal_conv_rows(xm, prev8, taps, cb_ref[0:1, cols])
        xm_b = xm.astype(BF16)
        xc_b = (y * jax.nn.sigmoid(y)).astype(BF16)
        xc_ref[0, :, cols] = xc_b

        for g in range(c * nc // BD_GROUP, (c + 1) * nc // BD_GROUP):
            gcols = slice(g * BD_GROUP, (g + 1) * BD_GROUP)
            sub = slice(g * BD_GROUP - c * nc, (g + 1) * BD_GROUP - c * nc)
            xc_g = xc_b[:, sub]
            q = jnp.dot(xc_g, wq_ref[g], preferred_element_type=F32)
            k = jnp.dot(xc_g, wk_ref[g], preferred_element_type=F32)
            v = jnp.dot(xm_b[:, sub], wv_ref[g], preferred_element_type=F32)
            qb, kb, vb = q.astype(BF16), k.astype(BF16), v.astype(BF16)
            gates = gates + jnp.dot(qb, wif_ref[0, gcols, :], preferred_element_type=F32)
            gates = gates + jnp.dot(kb, wif_ref[1, gcols, :], preferred_element_type=F32)
            gates = gates + jnp.dot(vb, wif_ref[2, gcols, :], preferred_element_type=F32)
            q_ref[0, :, gcols] = qb
            k_ref[0, :, gcols] = (k * k_scale).astype(BF16)
            k_t = lax.dot_general(wkt_ref[g], xc_g, (((1,), (1,)), ((), ())),
                                  preferred_element_type=F32)
            kt_ref[0, gcols, :] = (k_t * k_scale).astype(BF16)
            v_ref[0, :, gcols] = vb

        zcols = slice(inner + c * nc, inner + (c + 1) * nc)
        z_ref[0, :, cols] = jnp.dot(h_scr[...], wup_ref[:, zcols],
                                    preferred_element_type=F32).astype(BF16)

    lane = lax.broadcasted_iota(jnp.int32, gates.shape, 1)
    log_f = jax.nn.log_sigmoid(gates)
    csum = jnp.zeros((tm, GATE_LANES), F32)
    for piece in _split3(log_f):
        csum = csum + jnp.dot(tri_ref[...], piece, preferred_element_type=F32)
    gi = jnp.where(lane < n_heads, csum, gates)
    gcol = jnp.broadcast_to(cones_ref[...], (tm, GATE_LANES))
    grow = jnp.concatenate([rones_ref[...]] * (tm // GATE_LANES), axis=1)
    for j, piece in enumerate(_split3(gi)):
        gcol = gcol + jnp.dot(piece, cplace_ref[j], preferred_element_type=F32)
        grow = grow + lax.dot_general(rplace_ref[j], piece, (((1,), (1,)), ((), ())),
                                      preferred_element_type=F32)
    gcol_ref[0] = gcol.astype(BF16)
    grow_ref[0] = grow.astype(BF16)


def _blockdiag_groups(w):
    per = BD_GROUP // QKV_BLOCK
    wg = w.reshape(-1, per, QKV_BLOCK, QKV_BLOCK)
    eye = jnp.eye(per, dtype=w.dtype)
    dense = jnp.einsum("gnio,nm->gnimo", wg, eye)
    return dense.reshape(-1, BD_GROUP, BD_GROUP).astype(BF16)


def _mfront_call(x, mod, norm_g, w_up_all, conv_w, conv_b, wq, wk, wv, w_if, b_if, layer,
                 tm=TM_MFRONT):
    b, s, d = x.shape
    inner = w_up_all.shape[2] // 2
    n_heads = MLSTM_HEADS
    assert s % tm == 0 and tm % CHUNK_LEN == 0 and 2 * n_heads == 8
    dh = inner // n_heads
    w_if3 = w_if.reshape(3, inner, 2 * n_heads)
    w_if3 = jnp.concatenate([w_if3[..., n_heads:], w_if3[..., :n_heads]], axis=-1)
    w_if3 = jnp.pad(w_if3, ((0, 0), (0, 0), (0, GATE_LANES - 2 * n_heads))).astype(BF16)
    b_if2 = jnp.concatenate([b_if[n_heads:], b_if[:n_heads]])
    b_if2 = jnp.pad(b_if2, (0, GATE_LANES - 2 * n_heads)).reshape(1, GATE_LANES)
    n_groups = inner // BD_GROUP
    wk_groups = _blockdiag_groups(wk)
    col_place, row_place, col_ones, row_ones, _, tri = _gate_factor_constants(tm, CHUNK_LEN)

    row_spec = pl.BlockSpec((1, tm, d), lambda i, j: (i, j, 0))
    wide_spec = pl.BlockSpec((1, tm, inner), lambda i, j: (i, j, 0))
    wide_shape = jax.ShapeDtypeStruct((b, s, inner), BF16)
    return pl.pallas_call(
        functools.partial(_mfront_kernel, tm=tm, inner=inner, nc=512, k_scale=float(dh) ** -0.5),
        grid=(b, s // tm),
        in_specs=[
            row_spec,
            pl.BlockSpec((1, 6, d), lambda i, j: (i, 0, 0)),
            _resident((1, d)),
            _resident_layer(w_up_all, layer),
            _resident(conv_w.shape),
            _resident((1, inner)),
            _resident((n_groups, BD_GROUP, BD_GROUP)),
            _resident((n_groups, BD_GROUP, BD_GROUP)),
            _resident((n_groups, BD_GROUP, BD_GROUP)),
            _resident((n_groups, BD_GROUP, BD_GROUP)),
            _resident((3, inner, GATE_LANES)),
            _resident((1, GATE_LANES)),
            _resident((3, GATE_LANES, GATE_LANES)),
            _resident((3, GATE_LANES, GATE_LANES)),
            _resident((1, GATE_LANES)),
            _resident((GATE_LANES, GATE_LANES)),
            _resident((tm, tm)),
        ],
        out_specs=[
            wide_spec, wide_spec,
            pl.BlockSpec((1, inner, tm), lambda i, j: (i, 0, j)),
            wide_spec, wide_spec, wide_spec,
            pl.BlockSpec((1, tm, GATE_LANES), lambda i, j: (i, j, 0)),
            pl.BlockSpec((1, GATE_LANES, tm), lambda i, j: (i, 0, j)),
        ],
        out_shape=[
            wide_shape, wide_shape,
            jax.ShapeDtypeStruct((b, inner, s), BF16),
            wide_shape, wide_shape, wide_shape,
            jax.ShapeDtypeStruct((b, s, GATE_LANES), BF16),
            jax.ShapeDtypeStruct((b, GATE_LANES, s), BF16),
        ],
        scratch_shapes=[
            pltpu.VMEM((tm, d), BF16),
            pltpu.VMEM((8, inner), F32),
        ],
        compiler_params=_params(2),
        name="mlstm_front",
    )(x, mod, norm_g.reshape(1, d), w_up_all, conv_w, conv_b.reshape(1, inner),
      _blockdiag_groups(wq), wk_groups, jnp.swapaxes(wk_groups, 1, 2), _blockdiag_groups(wv),
      w_if3, b_if2, col_place, row_place, col_ones, row_ones, tri)


def _lanes(a, n):
    return jnp.concatenate([a] * n, axis=1)


def _mchunk_kernel(q_ref, k_ref, kt_ref, v_ref, xc_ref, z_ref, gcol_ref, grow_ref, expand_ref,
                   x_ref, mod_ref, lnw_ref, skip_ref, wdn_ref, o_ref, c_scr, m_scr, hn_scr, *,
                   ln, dh):
    n_heads = MLSTM_HEADS
    rep = GATE_LANES
    n_rep = dh // rep

    @pl.when(pl.program_id(1) == 0)
    def _():
        c_scr[...] = jnp.zeros_like(c_scr)
        m_scr[...] = jnp.zeros_like(m_scr)

    causal = (lax.broadcasted_iota(jnp.int32, (ln, ln), 0)
              >= lax.broadcasted_iota(jnp.int32, (ln, ln), 1))
    group = lax.broadcasted_iota(jnp.int32, (ln, rep), 1) // GATE_GROUP
    gcol = gcol_ref[0]
    for h in range(n_heads):
        cols = slice(h * dh, (h + 1) * dh)
        q, k, v = q_ref[0, :, cols], k_ref[0, :, cols], v_ref[0, :, cols]
        gcol_h = jnp.where(group == h, gcol, jnp.zeros_like(gcol))
        dmat = jnp.dot(gcol_h, grow_ref[0], preferred_element_type=F32)
        gi = jnp.dot(gcol_h, expand_ref[...], preferred_element_type=F32)
        g, i_minus_g = gi[:, 0:rep], gi[:, rep:2 * rep]
        m = m_scr[h, 0:1, :]
        g_tot = g[ln - 1:ln, :]

        dmat = jnp.where(causal, dmat, -jnp.inf)
        inter = m + g
        m_row = jnp.maximum(inter, jnp.max(dmat, axis=-1, keepdims=True))
        w_intra = jnp.exp(dmat - _lanes(m_row, ln // rep))
        w_inter = jnp.exp(inter - m_row)
        s = lax.dot_general(q, k, (((1,), (1,)), ((), ())), preferred_element_type=F32) * w_intra
        c_t = c_scr[h]
        qc = jnp.dot(q, c_t.astype(BF16), preferred_element_type=F32)
        num = (_lanes(w_inter, n_rep) * qc[:, 0:dh]
               + jnp.dot(s.astype(BF16), v, preferred_element_type=F32))
        den = w_inter * qc[:, dh:dh + rep] + jnp.sum(s, axis=-1, keepdims=True)
        inv = 1.0 / jnp.maximum(jnp.abs(den), jnp.exp(-m_row))
        hh = num * _lanes(inv, n_rep)
        mu = jnp.mean(hh, axis=-1, keepdims=True)
        var = jnp.mean(jnp.square(hh - mu), axis=-1, keepdims=True)
        hn_scr[:, cols] = (hh - mu) * lax.rsqrt(var + EPS)

        decay = i_minus_g + g_tot
        m_new = jnp.maximum(m + g_tot, jnp.max(decay, axis=0, keepdims=True))
        w_s = jnp.exp(decay - m_new)
        w_c = jnp.exp(m + g_tot - m_new)
        vw = jnp.concatenate(
            [(_lanes(w_s, n_rep) * v.astype(F32)).astype(BF16), w_s.astype(BF16)], axis=1)
        d_c = jnp.dot(kt_ref[0, cols, :], vw, preferred_element_type=F32)
        c_scr[h] = _lanes(w_c, n_rep + 1) * c_t + d_c
        m_scr[h] = jnp.broadcast_to(m_new, m_scr.shape[1:])

    z = z_ref[0].astype(F32)
    pre = ((hn_scr[...] * lnw_ref[...] + skip_ref[...] * xc_ref[0].astype(F32))
           * (z * jax.nn.sigmoid(z)))
    y = jnp.dot(pre.astype(BF16), wdn_ref[...], preferred_element_type=F32)
    o_ref[0] = x_ref[0] + mod_ref[0, 2:3, :] * y


def _mchunk_call(q, k, k_t, v, xc, z, gcol, grow, x, mod, ln_w, skip, w_down_all, layer):
    b, s, d = x.shape
    inner = q.shape[-1]
    n_heads = MLSTM_HEADS
    dh = inner // n_heads
    ln = CHUNK_LEN
    wide_spec = pl.BlockSpec((1, ln, inner), lambda i, j: (i, j, 0))
    row_spec = pl.BlockSpec((1, ln, d), lambda i, j: (i, j, 0))
    return pl.pallas_call(
        functools.partial(_mchunk_kernel, ln=ln, dh=dh),
        grid=(b, s // ln),
        in_specs=[
            wide_spec, wide_spec,
            pl.BlockSpec((1, inner, ln), lambda i, j: (i, 0, j)),
            wide_spec, wide_spec, wide_spec,
            pl.BlockSpec((1, ln, GATE_LANES), lambda i, j: (i, j, 0)),
            pl.BlockSpec((1, GATE_LANES, ln), lambda i, j: (i, 0, j)),
            _resident((GATE_LANES, 2 * GATE_LANES)),
            row_spec,
            pl.BlockSpec((1, 6, d), lambda i, j: (i, 0, 0)),
            _resident((1, inner)),
            _resident((1, inner)),
            _resident_layer(w_down_all, layer),
        ],
        out_specs=row_spec,
        out_shape=jax.ShapeDtypeStruct((b, s, d), F32),
        scratch_shapes=[
            pltpu.VMEM((n_heads, dh, dh + GATE_LANES), F32),
            pltpu.VMEM((n_heads, 8, GATE_LANES), F32),
            pltpu.VMEM((ln, inner), F32),
        ],
        compiler_params=_params(2),
        name="mlstm_chunk",
    )(q, k, k_t, v, xc, z, gcol, grow, _gate_factor_constants(ln, ln)[4], x, mod,
      ln_w.reshape(1, inner), skip.reshape(1, inner), w_down_all)


def kernel(x, c, ada_w, ada_b, norm_mix_g, norm_ffn_g, final_g, mlstm_w_up, mlstm_conv_w,
           mlstm_conv_b, mlstm_wq, mlstm_wk, mlstm_wv, mlstm_w_if, mlstm_b_if, mlstm_ln_w,
           mlstm_skip, mlstm_w_down, conf_w_pw1, conf_b_pw1, conf_dw_w, conf_dw_b, conf_ln_g,
           conf_ln_b, conf_w_pw2, conf_b_pw2, ffn_w_up, ffn_conv_w, ffn_conv_b, ffn_w_down):
    depth = ada_w.shape[0]
    b, _, d = x.shape
    mods = _mods_call(c, ada_w, ada_b).reshape(depth, b, 6, d)
    mlstm_w_up, mlstm_w_down = mlstm_w_up.astype(BF16), mlstm_w_down.astype(BF16)
    conf_w_pw1, conf_w_pw2 = conf_w_pw1.astype(BF16), conf_w_pw2.astype(BF16)
    ffn_w_up, ffn_w_down = ffn_w_up.astype(BF16), ffn_w_down.astype(BF16)
    for layer in range(depth):
        mod = mods[layer]
        j = layer // 2
        if layer % 2 == 0:
            q, k, k_t, v, xc, z, gcol, grow = _mfront_call(
                x, mod, norm_mix_g[layer], mlstm_w_up, mlstm_conv_w[j], mlstm_conv_b[j],
                mlstm_wq[j], mlstm_wk[j], mlstm_wv[j], mlstm_w_if[j], mlstm_b_if[j], j,
                tm=256 if j == 0 else 512)
            x = _mchunk_call(q, k, k_t, v, xc, z, gcol, grow, x, mod, mlstm_ln_w[j], mlstm_skip[j],
                             mlstm_w_down, j)
        else:
            tiling = dict(tm=512, rb=64, cw=128) if j == 0 else dict(tm=256, rb=32, cw=256, lane_pad=0)
            x = _conf_call(x, mod, norm_mix_g[layer], conf_w_pw1, conf_b_pw1[j], conf_dw_w[j],
                           conf_dw_b[j], conf_ln_g[j], conf_ln_b[j], conf_w_pw2, conf_b_pw2[j], j,
                           **tiling)
        ffn_tiling = [dict(), dict(tm=1024), dict(fc=1408), dict()][layer]
        x = _ffn_call(x, mod, norm_ffn_g[layer], ffn_w_up, ffn_conv_w[layer], ffn_conv_b[layer],
                      ffn_w_down, layer, final_g if layer == depth - 1 else None, **ffn_tiling)
    return x
```

```python
import functools

import jax
import jax.numpy as jnp
import numpy as np
from jax import lax
from jax.experimental import pallas as pl
from jax.experimental.pallas import tpu as pltpu

F32 = jnp.float32
BF16 = jnp.bfloat16
EPS = 1e-6

MLSTM_HEADS = 4
QKV_BLOCK = 4
GATE_LANES = 128
GATE_GROUP = 16
BD_GROUP = 256

TM_FFN = 1024
TM_CONF = 512
TM_MFRONT = 512
CHUNK_LEN = 256

VMEM_LIMIT_BYTES = 56 * 1024 * 1024


def _params(n_axes):
    return pltpu.CompilerParams(
        dimension_semantics=("arbitrary",) * n_axes,
        vmem_limit_bytes=VMEM_LIMIT_BYTES,
    )


def _resident(shape):
    zeros = (0,) * len(shape)
    return pl.BlockSpec(shape, lambda *_: zeros, pipeline_mode=pl.Buffered(1))


def _resident_layer(stacked, layer):
    shape = stacked.shape[1:]
    index = (layer,) + (0,) * len(shape)
    return pl.BlockSpec((None,) + shape, lambda *_: index, pipeline_mode=pl.Buffered(1))


def _rmsnorm(x, g):
    return x * lax.rsqrt(jnp.mean(x * x, axis=-1, keepdims=True) + EPS) * g


def _norm_modulate(x, g, shift, scale):
    return _rmsnorm(x, g) * (1.0 + scale) + shift


def _causal_conv_rows(u, prev8, taps, bias):
    top = u[0:8]
    row8 = lax.broadcasted_iota(jnp.int32, top.shape, 0)
    y = taps[0] * u + bias
    y_top = taps[0] * top + bias
    for j in range(1, len(taps)):
        y = y + taps[j] * pltpu.roll(u, j, 0)
        shifted_top = jnp.where(row8 < j, pltpu.roll(prev8, j, 0), pltpu.roll(top, j, 0))
        y_top = y_top + taps[j] * shifted_top
    return jnp.concatenate([y_top, y[8:]], axis=0)


def _mods_kernel(c_ref, w_ref, b_ref, o_ref):
    c = c_ref[...]
    c_act = (c * jax.nn.sigmoid(c)).astype(BF16)
    o_ref[0] = jnp.dot(c_act, w_ref[0].astype(BF16), preferred_element_type=F32) + b_ref[0]


def _mods_call(c, ada_w, ada_b):
    depth, d, n = ada_w.shape
    b = c.shape[0]
    tn = 1024
    return pl.pallas_call(
        _mods_kernel,
        grid=(depth, n // tn),
        in_specs=[
            pl.BlockSpec((b, d), lambda l, j: (0, 0)),
            pl.BlockSpec((1, d, tn), lambda l, j: (l, 0, j)),
            pl.BlockSpec((1, 1, tn), lambda l, j: (l, 0, j)),
        ],
        out_specs=pl.BlockSpec((1, b, tn), lambda l, j: (l, 0, j)),
        out_shape=jax.ShapeDtypeStruct((depth, b, n), F32),
        compiler_params=_params(2),
        name="adaln_mods",
    )(c, ada_w, ada_b.reshape(depth, 1, n))


def _ffn_kernel(x_ref, mod_ref, ng_ref, wup_ref, cw_ref, cb_ref, wdn_ref, *rest, tm, f_dim, fc,
                final):
    if final:
        fg_ref, o_ref, h_scr, act_scr, halo_scr = rest
    else:
        o_ref, h_scr, act_scr, halo_scr = rest

    @pl.when(pl.program_id(1) == 0)
    def _():
        halo_scr[...] = jnp.zeros_like(halo_scr)

    x = x_ref[0]
    shift, scale, gate = mod_ref[0, 3:4, :], mod_ref[0, 4:5, :], mod_ref[0, 5:6, :]
    h_scr[...] = _norm_modulate(x, ng_ref[...], shift, scale).astype(BF16)

    for f in range(f_dim // fc):
        halves = []
        for half in range(2):
            c0 = half * f_dim + f * fc
            cols = slice(c0, c0 + fc)
            u = jnp.dot(h_scr[...], wup_ref[:, cols], preferred_element_type=F32)
            prev8 = halo_scr[:, cols]
            halo_scr[:, cols] = u[tm - 8:tm]
            taps = (cw_ref[2:3, cols], cw_ref[1:2, cols], cw_ref[0:1, cols])
            halves.append(_causal_conv_rows(u, prev8, taps, cb_ref[0:1, cols]))
        yg, yv = halves
        act_scr[:, f * fc:(f + 1) * fc] = (yg * jax.nn.sigmoid(yg) * yv).astype(BF16)

    y = jnp.dot(act_scr[...], wdn_ref[...], preferred_element_type=F32)
    out = x + gate * y
    if final:
        out = _rmsnorm(out, fg_ref[...])
    o_ref[0] = out


def _ffn_call(x, mod, norm_g, w_up_all, conv_w, conv_b, w_down_all, layer, final_g=None,
              tm=TM_FFN, fc=256):
    b, s, d = x.shape
    f_dim = w_down_all.shape[1]
    assert s % tm == 0 and f_dim % fc == 0
    final = final_g is not None
    row_spec = pl.BlockSpec((1, tm, d), lambda i, j: (i, j, 0))
    in_specs = [
        row_spec,
        pl.BlockSpec((1, 6, d), lambda i, j: (i, 0, 0)),
        _resident((1, d)),
        _resident_layer(w_up_all, layer),
        _resident((3, 2 * f_dim)),
        _resident((1, 2 * f_dim)),
        _resident_layer(w_down_all, layer),
    ]
    args = [x, mod, norm_g.reshape(1, d), w_up_all, conv_w, conv_b.reshape(1, -1), w_down_all]
    if final:
        in_specs.append(_resident((1, d)))
        args.append(final_g.reshape(1, d))
    return pl.pallas_call(
        functools.partial(_ffn_kernel, tm=tm, f_dim=f_dim, fc=fc, final=final),
        grid=(b, s // tm),
        in_specs=in_specs,
        out_specs=row_spec,
        out_shape=jax.ShapeDtypeStruct((b, s, d), F32),
        scratch_shapes=[
            pltpu.VMEM((tm, d), BF16),
            pltpu.VMEM((tm, f_dim), BF16),
            pltpu.VMEM((8, 2 * f_dim), F32),
        ],
        compiler_params=_params(2),
        name="conv_ffn",
    )(*args)


CONF_PAD = 32
CONF_LANE_PAD = 128


def _conf_kernel(x_ref, mod_ref, ng_ref, w1_ref, b1_ref, dw_ref, db_ref, lg_ref, lb_ref, w2_ref,
                 b2_ref, o_ref, ext_scr, shift_scr, y_scr, taps_scr, *, tm, d, kw, rb, cw,
                 hoist_taps):
    @pl.when(pl.program_id(1) == 0)
    def _():
        ext_scr[0:CONF_PAD, 0:d] = jnp.zeros((CONF_PAD, d), F32)

    @pl.when((pl.program_id(0) == 0) & (pl.program_id(1) == 0))
    def _():
        for k in range(kw):
            taps_scr[k] = jnp.broadcast_to(dw_ref[k:k + 1, :], (8, d))

    x = x_ref[0]
    shift, scale, gate = mod_ref[0, 0:1, :], mod_ref[0, 1:2, :], mod_ref[0, 2:3, :]
    h = _norm_modulate(x, ng_ref[...], shift, scale).astype(BF16)
    a = jnp.dot(h, w1_ref[:, 0:d], preferred_element_type=F32) + b1_ref[0:1, 0:d]
    g = jnp.dot(h, w1_ref[:, d:2 * d], preferred_element_type=F32) + b1_ref[0:1, d:2 * d]
    ext_scr[CONF_PAD:CONF_PAD + tm, 0:d] = a * jax.nn.sigmoid(g)

    span = tm + CONF_PAD - 8
    ext = ext_scr[:, 0:d]
    for b in range(1, 8):
        shift_scr[b - 1, :, 0:d] = pltpu.roll(ext, tm + CONF_PAD - b, 0)[0:span]

    base = CONF_PAD - (kw - 1)

    def tap(k, cols):
        return jnp.concatenate([taps_scr[k, :, cols]] * (rb // 8), axis=0)

    for c in range(d // cw):
        cols = slice(c * cw, (c + 1) * cw)
        taps = [tap(k, cols) for k in range(kw)] if hoist_taps else None
        bias = jnp.broadcast_to(db_ref[0:1, cols], (rb, cw))

        def row_block(i, carry, cols=cols, taps=taps, bias=bias):
            r0 = pl.multiple_of(i * rb, rb)
            acc = bias
            for k in range(kw):
                q8, b = divmod(base + k, 8)
                if b == 0:
                    win = ext_scr[pl.ds(r0 + 8 * q8, rb), cols]
                else:
                    win = shift_scr[b - 1, pl.ds(r0 + 8 * q8, rb), cols]
                acc = acc + (taps[k] if hoist_taps else tap(k, cols)) * win
            y_scr[pl.ds(r0, rb), cols] = acc
            return carry

        lax.fori_loop(0, tm // rb, row_block, 0)
    ext_scr[0:CONF_PAD, 0:d] = ext_scr[tm:tm + CONF_PAD, 0:d]

    y = y_scr[...]
    mu = jnp.mean(y, axis=-1, keepdims=True)
    var = jnp.mean(jnp.square(y - mu), axis=-1, keepdims=True)
    ln = (y - mu) * lax.rsqrt(var + EPS) * lg_ref[...] + lb_ref[...]
    act = (ln * jax.nn.sigmoid(ln)).astype(BF16)
    out = jnp.dot(act, w2_ref[...], preferred_element_type=F32) + b2_ref[...]
    o_ref[0] = x + gate * out


def _conf_call(x, mod, norm_g, w_pw1_all, b_pw1, dw_w, dw_b, ln_g, ln_b, w_pw2_all, b_pw2, layer,
               tm=TM_CONF, rb=64, cw=128, lane_pad=CONF_LANE_PAD):
    b, s, d = x.shape
    kw = dw_w.shape[0]
    assert s % tm == 0 and kw - 1 <= CONF_PAD
    row_spec = pl.BlockSpec((1, tm, d), lambda i, j: (i, j, 0))
    return pl.pallas_call(
        functools.partial(_conf_kernel, tm=tm, d=d, kw=kw, rb=rb, cw=cw, hoist_taps=cw <= 128),
        grid=(b, s // tm),
        in_specs=[
            row_spec,
            pl.BlockSpec((1, 6, d), lambda i, j: (i, 0, 0)),
            _resident((1, d)),
            _resident_layer(w_pw1_all, layer),
            _resident((1, 2 * d)),
            _resident((kw, d)),
            _resident((1, d)),
            _resident((1, d)),
            _resident((1, d)),
            _resident_layer(w_pw2_all, layer),
            _resident((1, d)),
        ],
        out_specs=row_spec,
        out_shape=jax.ShapeDtypeStruct((b, s, d), F32),
        scratch_shapes=[
            pltpu.VMEM((tm + CONF_PAD, d + lane_pad), F32),
            pltpu.VMEM((7, tm + CONF_PAD - 8, d + lane_pad), F32),
            pltpu.VMEM((tm, d), F32),
            pltpu.VMEM((kw, 8, d), F32),
        ],
        compiler_params=_params(2),
        name="conformer_conv",
    )(x, mod, norm_g.reshape(1, d), w_pw1_all, b_pw1.reshape(1, -1), dw_w, dw_b.reshape(1, d),
      ln_g.reshape(1, d), ln_b.reshape(1, d), w_pw2_all, b_pw2.reshape(1, d))


def _split3(x):
    hi = x.astype(BF16)
    rest = x - hi.astype(F32)
    mid = rest.astype(BF16)
    lo = (rest - mid.astype(F32)).astype(BF16)
    return hi, mid, lo


def _gate_factor_constants(tm, seg):
    gl, nh, n = GATE_GROUP, MLSTM_HEADS, GATE_LANES
    col_place = np.zeros((3, n, n), np.float32)
    row_place = np.zeros((3, n, n), np.float32)
    col_ones = np.zeros((1, n), np.float32)
    row_ones = np.zeros((n, n), np.float32)
    expand = np.zeros((n, 2 * n), np.float32)
    for h in range(nh):
        for j in range(3):
            col_place[j, h, gl * h + j] = 1.0
            col_place[j, nh + h, gl * h + 9 + j] = 1.0
            row_place[j, gl * h + 3 + j, nh + h] = 1.0
            row_place[j, gl * h + 6 + j, h] = -1.0
            col_ones[0, gl * h + 3 + j] = 1.0
            col_ones[0, gl * h + 6 + j] = 1.0
            row_ones[gl * h + j, :] = 1.0
            expand[gl * h + j, :n] = 1.0
            expand[gl * h + j, n:] = -1.0
            expand[gl * h + 9 + j, n:] = 1.0
    pos = np.arange(tm)
    tri = ((pos[:, None] >= pos[None, :]) & (pos[:, None] // seg == pos[None, :] // seg))
    as_bf16 = lambda a: jnp.asarray(a, BF16)
    return (as_bf16(col_place), as_bf16(row_place), jnp.asarray(col_ones), jnp.asarray(row_ones),
            as_bf16(expand), as_bf16(tri.astype(np.float32)))


def _mfront_kernel(x_ref, mod_ref, ng_ref, wup_ref, cw_ref, cb_ref, wq_ref, wkt_ref, wv_ref,
                   wgc_ref, wgm_ref, bif_ref, cplace_ref, rplace_ref, cones_ref, rones_ref, tri_ref,
                   q_ref, kt_ref, v_ref, xc_ref, z_ref, gcol_ref, grow_ref,
                   h_scr, halo_scr, *, tm, inner, nc, k_scale):
    n_heads = MLSTM_HEADS

    @pl.when(pl.program_id(1) == 0)
    def _():
        halo_scr[...] = jnp.zeros_like(halo_scr)

    x = x_ref[0]
    shift, scale = mod_ref[0, 0:1, :], mod_ref[0, 1:2, :]
    h_scr[...] = _norm_modulate(x, ng_ref[...], shift, scale).astype(BF16)

    kw = cw_ref.shape[0]
    gates = jnp.broadcast_to(bif_ref[...], (tm, GATE_LANES))
    n_chunks = inner // nc
    xm_next = jnp.dot(h_scr[...], wup_ref[:, 0:nc], preferred_element_type=F32)
    for c in range(n_chunks):
        cols = slice(c * nc, (c + 1) * nc)
        xm = xm_next
        if c + 1 < n_chunks:
            xm_next = jnp.dot(h_scr[...], wup_ref[:, (c + 1) * nc:(c + 2) * nc],
                              preferred_element_type=F32)
        prev8 = halo_scr[:, cols]
        halo_scr[:, cols] = xm[tm - 8:tm]
        taps = tuple(cw_ref[kw - 1 - j:kw - j, cols] for j in range(kw))
        y = _causal_conv_rows(xm, prev8, taps, cb_ref[0:1, cols])
        xm_b = xm.astype(BF16)
        xc_b = (y * jax.nn.sigmoid(y)).astype(BF16)
        xc_ref[0, :, cols] = xc_b

        for g in range(c * nc // BD_GROUP, (c + 1) * nc // BD_GROUP):
            gcols = slice(g * BD_GROUP, (g + 1) * BD_GROUP)
            sub = slice(g * BD_GROUP - c * nc, (g + 1) * BD_GROUP - c * nc)
            xc_g, xm_g = xc_b[:, sub], xm_b[:, sub]
            q = jnp.dot(xc_g, wq_ref[g], preferred_element_type=F32)
            v = jnp.dot(xm_g, wv_ref[g], preferred_element_type=F32)
            k_t = lax.dot_general(wkt_ref[g], xc_g, (((1,), (1,)), ((), ())),
                                  preferred_element_type=F32)
            gates = gates + jnp.dot(xc_g, wgc_ref[g], preferred_element_type=F32)
            gates = gates + jnp.dot(xm_g, wgm_ref[g], preferred_element_type=F32)
            q_ref[0, :, gcols] = q.astype(BF16)
            kt_ref[0, gcols, :] = (k_t * k_scale).astype(BF16)
            v_ref[0, :, gcols] = v.astype(BF16)

        zcols = slice(inner + c * nc, inner + (c + 1) * nc)
        z_ref[0, :, cols] = jnp.dot(h_scr[...], wup_ref[:, zcols],
                                    preferred_element_type=F32).astype(BF16)

    lane = lax.broadcasted_iota(jnp.int32, gates.shape, 1)
    log_f = jax.nn.log_sigmoid(gates)
    csum = jnp.zeros((tm, GATE_LANES), F32)
    for piece in _split3(log_f):
        csum = csum + jnp.dot(tri_ref[...], piece, preferred_element_type=F32)
    gi = jnp.where(lane < n_heads, csum, gates)
    gcol = jnp.broadcast_to(cones_ref[...], (tm, GATE_LANES))
    grow = jnp.concatenate([rones_ref[...]] * (tm // GATE_LANES), axis=1)
    for j, piece in enumerate(_split3(gi)):
        gcol = gcol + jnp.dot(piece, cplace_ref[j], preferred_element_type=F32)
        grow = grow + lax.dot_general(rplace_ref[j], piece, (((1,), (1,)), ((), ())),
                                      preferred_element_type=F32)
    gcol_ref[0] = gcol.astype(BF16)
    grow_ref[0] = grow.astype(BF16)


def _qkv_prep_kernel(wq_ref, wk_ref, wkt_ref, wv_ref, wif_ref, dq_ref, dkt_ref, dv_ref, gc_ref,
                     gm_ref):
    row = lax.broadcasted_iota(jnp.int32, (BD_GROUP, BD_GROUP), 0)
    col = lax.broadcasted_iota(jnp.int32, (BD_GROUP, BD_GROUP), 1)
    same_block = (row // QKV_BLOCK) == (col // QKV_BLOCK)

    def dense(w_ref):
        w = w_ref[...]
        tiled = jnp.zeros((BD_GROUP, BD_GROUP), F32)
        for o in range(QKV_BLOCK):
            tiled = jnp.where(col % QKV_BLOCK == o, w[:, o:o + 1], tiled)
        return jnp.where(same_block, tiled, 0.0)

    dq, dk, dv = dense(wq_ref), dense(wk_ref), dense(wv_ref)
    dq_ref[...] = dq.astype(BF16)
    dkt_ref[...] = dense(wkt_ref).astype(BF16)
    dv_ref[...] = dv.astype(BF16)
    hi = lax.Precision.HIGHEST
    gc = (jnp.dot(dq, wif_ref[0], preferred_element_type=F32, precision=hi)
          + jnp.dot(dk, wif_ref[1], preferred_element_type=F32, precision=hi))
    gc_ref[...] = gc.astype(BF16)
    gm_ref[...] = jnp.dot(dv, wif_ref[2], preferred_element_type=F32, precision=hi).astype(BF16)


def _qkv_prep_call(wq, wk, wv, w_if):
    n_layers, n_blocks = wq.shape[0], wq.shape[1]
    n_heads = MLSTM_HEADS
    inner = n_blocks * QKV_BLOCK
    n_groups = inner // BD_GROUP
    rows = lambda w: w.reshape(n_layers, n_groups, BD_GROUP, QKV_BLOCK)
    w_if4 = w_if.reshape(n_layers, 3, inner, 2 * n_heads)
    w_if4 = jnp.concatenate([w_if4[..., n_heads:], w_if4[..., :n_heads]], axis=-1)
    w_if4 = jnp.pad(w_if4, ((0, 0), (0, 0), (0, 0), (0, GATE_LANES - 2 * n_heads)))
    w_spec = pl.BlockSpec((None, None, BD_GROUP, QKV_BLOCK), lambda l, g: (l, g, 0, 0))
    dense_spec = pl.BlockSpec((None, None, BD_GROUP, BD_GROUP), lambda l, g: (l, g, 0, 0))
    gate_spec = pl.BlockSpec((None, None, BD_GROUP, GATE_LANES), lambda l, g: (l, g, 0, 0))
    dense_shape = jax.ShapeDtypeStruct((n_layers, n_groups, BD_GROUP, BD_GROUP), BF16)
    gate_shape = jax.ShapeDtypeStruct((n_layers, n_groups, BD_GROUP, GATE_LANES), BF16)
    return pl.pallas_call(
        _qkv_prep_kernel,
        grid=(n_layers, n_groups),
        in_specs=[w_spec, w_spec, w_spec, w_spec,
                  pl.BlockSpec((None, 3, BD_GROUP, GATE_LANES), lambda l, g: (l, 0, g, 0))],
        out_specs=[dense_spec, dense_spec, dense_spec, gate_spec, gate_spec],
        out_shape=[dense_shape, dense_shape, dense_shape, gate_shape, gate_shape],
        compiler_params=_params(2),
        name="qkv_weight_prep",
    )(rows(wq), rows(wk), rows(jnp.swapaxes(wk, -1, -2)), rows(wv), w_if4)


def _mfront_call(x, mod, norm_g, w_up_all, conv_w, conv_b, qkv_weights, b_if, layer, tm=TM_MFRONT):
    b, s, d = x.shape
    inner = w_up_all.shape[2] // 2
    n_heads = MLSTM_HEADS
    assert s % tm == 0 and tm % CHUNK_LEN == 0 and 2 * n_heads == 8
    dh = inner // n_heads
    b_if2 = jnp.concatenate([b_if[n_heads:], b_if[:n_heads]])
    b_if2 = jnp.pad(b_if2, (0, GATE_LANES - 2 * n_heads)).reshape(1, GATE_LANES)
    col_place, row_place, col_ones, row_ones, _, tri = _gate_factor_constants(tm, CHUNK_LEN)

    row_spec = pl.BlockSpec((1, tm, d), lambda i, j: (i, j, 0))
    wide_spec = pl.BlockSpec((1, tm, inner), lambda i, j: (i, j, 0))
    wide_shape = jax.ShapeDtypeStruct((b, s, inner), BF16)
    return pl.pallas_call(
        functools.partial(_mfront_kernel, tm=tm, inner=inner, nc=512, k_scale=float(dh) ** -0.5),
        grid=(b, s // tm),
        in_specs=[
            row_spec,
            pl.BlockSpec((1, 6, d), lambda i, j: (i, 0, 0)),
            _resident((1, d)),
            _resident_layer(w_up_all, layer),
            _resident(conv_w.shape),
            _resident((1, inner)),
            *[_resident_layer(w, layer) for w in qkv_weights],
            _resident((1, GATE_LANES)),
            _resident((3, GATE_LANES, GATE_LANES)),
            _resident((3, GATE_LANES, GATE_LANES)),
            _resident((1, GATE_LANES)),
            _resident((GATE_LANES, GATE_LANES)),
            _resident((tm, tm)),
        ],
        out_specs=[
            wide_spec,
            pl.BlockSpec((1, inner, tm), lambda i, j: (i, 0, j)),
            wide_spec, wide_spec, wide_spec,
            pl.BlockSpec((1, tm, GATE_LANES), lambda i, j: (i, j, 0)),
            pl.BlockSpec((1, GATE_LANES, tm), lambda i, j: (i, 0, j)),
        ],
        out_shape=[
            wide_shape,
            jax.ShapeDtypeStruct((b, inner, s), BF16),
            wide_shape, wide_shape, wide_shape,
            jax.ShapeDtypeStruct((b, s, GATE_LANES), BF16),
            jax.ShapeDtypeStruct((b, GATE_LANES, s), BF16),
        ],
        scratch_shapes=[
            pltpu.VMEM((tm, d), BF16),
            pltpu.VMEM((8, inner), F32),
        ],
        compiler_params=_params(2),
        name="mlstm_front",
    )(x, mod, norm_g.reshape(1, d), w_up_all, conv_w, conv_b.reshape(1, inner), *qkv_weights,
      b_if2, col_place, row_place, col_ones, row_ones, tri)


def _lanes(a, n):
    return jnp.concatenate([a] * n, axis=1)


def _mchunk_kernel(q_ref, kt_ref, v_ref, xc_ref, z_ref, gcol_ref, grow_ref, expand_ref,
                   x_ref, mod_ref, lnw_ref, skip_ref, wdn_ref, o_ref, c_scr, m_scr, hn_scr, *,
                   ln, dh):
    n_heads = MLSTM_HEADS
    rep = GATE_LANES
    n_rep = dh // rep

    @pl.when(pl.program_id(1) == 0)
    def _():
        c_scr[...] = jnp.zeros_like(c_scr)
        m_scr[...] = jnp.zeros_like(m_scr)

    causal = (lax.broadcasted_iota(jnp.int32, (ln, ln), 0)
              >= lax.broadcasted_iota(jnp.int32, (ln, ln), 1))
    group = lax.broadcasted_iota(jnp.int32, (ln, rep), 1) // GATE_GROUP
    gcol = gcol_ref[0]
    for h in range(n_heads):
        cols = slice(h * dh, (h + 1) * dh)
        q, k_t, v = q_ref[0, :, cols], kt_ref[0, cols, :], v_ref[0, :, cols]
        gcol_h = jnp.where(group == h, gcol, jnp.zeros_like(gcol))
        dmat = jnp.dot(gcol_h, grow_ref[0], preferred_element_type=F32)
        gi = jnp.dot(gcol_h, expand_ref[...], preferred_element_type=F32)
        g, i_minus_g = gi[:, 0:rep], gi[:, rep:2 * rep]
        m = m_scr[h, 0:1, :]
        g_tot = g[ln - 1:ln, :]

        dmat = jnp.where(causal, dmat, -jnp.inf)
        inter = m + g
        m_row = jnp.maximum(inter, jnp.max(dmat, axis=-1, keepdims=True))
        w_intra = jnp.exp(dmat - _lanes(m_row, ln // rep))
        w_inter = jnp.exp(inter - m_row)
        s = jnp.dot(q, k_t, preferred_element_type=F32) * w_intra
        c_t = c_scr[h]
        qc = jnp.dot(q, c_t.astype(BF16), preferred_element_type=F32)
        num = (_lanes(w_inter, n_rep) * qc[:, 0:dh]
               + jnp.dot(s.astype(BF16), v, preferred_element_type=F32))
        den = w_inter * qc[:, dh:dh + rep] + jnp.sum(s, axis=-1, keepdims=True)
        inv = 1.0 / jnp.maximum(jnp.abs(den), jnp.exp(-m_row))
        hh = num * _lanes(inv, n_rep)
        mu = jnp.mean(hh, axis=-1, keepdims=True)
        var = jnp.mean(jnp.square(hh - mu), axis=-1, keepdims=True)
        hn_scr[:, cols] = (hh - mu) * lax.rsqrt(var + EPS)

        decay = i_minus_g + g_tot
        m_new = jnp.maximum(m + g_tot, jnp.max(decay, axis=0, keepdims=True))
        w_s = jnp.exp(decay - m_new)
        w_c = jnp.exp(m + g_tot - m_new)
        vw = jnp.concatenate(
            [(_lanes(w_s, n_rep) * v.astype(F32)).astype(BF16), w_s.astype(BF16)], axis=1)
        d_c = jnp.dot(k_t, vw, preferred_element_type=F32)
        c_scr[h] = _lanes(w_c, n_rep + 1) * c_t + d_c
        m_scr[h] = jnp.broadcast_to(m_new, m_scr.shape[1:])

    z = z_ref[0].astype(F32)
    pre = ((hn_scr[...] * lnw_ref[...] + skip_ref[...] * xc_ref[0].astype(F32))
           * (z * jax.nn.sigmoid(z)))
    y = jnp.dot(pre.astype(BF16), wdn_ref[...], preferred_element_type=F32)
    o_ref[0] = x_ref[0] + mod_ref[0, 2:3, :] * y


def _mchunk_call(q, k_t, v, xc, z, gcol, grow, x, mod, ln_w, skip, w_down_all, layer):
    b, s, d = x.shape
    inner = q.shape[-1]
    n_heads = MLSTM_HEADS
    dh = inner // n_heads
    ln = CHUNK_LEN
    wide_spec = pl.BlockSpec((1, ln, inner), lambda i, j: (i, j, 0))
    row_spec = pl.BlockSpec((1, ln, d), lambda i, j: (i, j, 0))
    return pl.pallas_call(
        functools.partial(_mchunk_kernel, ln=ln, dh=dh),
        grid=(b, s // ln),
        in_specs=[
            wide_spec,
            pl.BlockSpec((1, inner, ln), lambda i, j: (i, 0, j)),
            wide_spec, wide_spec, wide_spec,
            pl.BlockSpec((1, ln, GATE_LANES), lambda i, j: (i, j, 0)),
            pl.BlockSpec((1, GATE_LANES, ln), lambda i, j: (i, 0, j)),
            _resident((GATE_LANES, 2 * GATE_LANES)),
            row_spec,
            pl.BlockSpec((1, 6, d), lambda i, j: (i, 0, 0)),
            _resident((1, inner)),
            _resident((1, inner)),
            _resident_layer(w_down_all, layer),
        ],
        out_specs=row_spec,
        out_shape=jax.ShapeDtypeStruct((b, s, d), F32),
        scratch_shapes=[
            pltpu.VMEM((n_heads, dh, dh + GATE_LANES), F32),
            pltpu.VMEM((n_heads, 8, GATE_LANES), F32),
            pltpu.VMEM((ln, inner), F32),
        ],
        compiler_params=_params(2),
        name="mlstm_chunk",
    )(q, k_t, v, xc, z, gcol, grow, _gate_factor_constants(ln, ln)[4], x, mod,
      ln_w.reshape(1, inner), skip.reshape(1, inner), w_down_all)


def kernel(x, c, ada_w, ada_b, norm_mix_g, norm_ffn_g, final_g, mlstm_w_up, mlstm_conv_w,
           mlstm_conv_b, mlstm_wq, mlstm_wk, mlstm_wv, mlstm_w_if, mlstm_b_if, mlstm_ln_w,
           mlstm_skip, mlstm_w_down, conf_w_pw1, conf_b_pw1, conf_dw_w, conf_dw_b, conf_ln_g,
           conf_ln_b, conf_w_pw2, conf_b_pw2, ffn_w_up, ffn_conv_w, ffn_conv_b, ffn_w_down):
    depth = ada_w.shape[0]
    b, _, d = x.shape
    mods = _mods_call(c, ada_w, ada_b).reshape(depth, b, 6, d)
    mlstm_w_up, mlstm_w_down = mlstm_w_up.astype(BF16), mlstm_w_down.astype(BF16)
    conf_w_pw1, conf_w_pw2 = conf_w_pw1.astype(BF16), conf_w_pw2.astype(BF16)
    ffn_w_up, ffn_w_down = ffn_w_up.astype(BF16), ffn_w_down.astype(BF16)
    qkv_weights = _qkv_prep_call(mlstm_wq, mlstm_wk, mlstm_wv, mlstm_w_if)
    for layer in range(depth):
        mod = mods[layer]
        j = layer // 2
        if layer % 2 == 0:
            q, k_t, v, xc, z, gcol, grow = _mfront_call(
                x, mod, norm_mix_g[layer], mlstm_w_up, mlstm_conv_w[j], mlstm_conv_b[j],
                qkv_weights, mlstm_b_if[j], j)
            x = _mchunk_call(q, k_t, v, xc, z, gcol, grow, x, mod, mlstm_ln_w[j], mlstm_skip[j],
                             mlstm_w_down, j)
        else:
            x = _conf_call(x, mod, norm_mix_g[layer], conf_w_pw1, conf_b_pw1[j], conf_dw_w[j],
                           conf_dw_b[j], conf_ln_g[j], conf_ln_b[j], conf_w_pw2, conf_b_pw2[j], j,
                           rb=64 if j == 0 else 128)
        x = _ffn_call(x, mod, norm_ffn_g[layer], ffn_w_up, ffn_conv_w[layer], ffn_conv_b[layer],
                      ffn_w_down, layer, final_g if layer == depth - 1 else None)
    return x
```

```python
import functools

import jax
import jax.numpy as jnp
import numpy as np
from jax import lax
from jax.experimental import pallas as pl
from jax.experimental.pallas import tpu as pltpu

F32 = jnp.float32
BF16 = jnp.bfloat16
EPS = 1e-6

MLSTM_HEADS = 4
QKV_BLOCK = 4
GATE_LANES = 128
GATE_GROUP = 16
BD_GROUP = 256

TM_FFN = 1024
TM_CONF = 512
TM_MFRONT = 512
CHUNK_LEN = 256

VMEM_LIMIT_BYTES = 56 * 1024 * 1024


def _params(n_axes):
    return pltpu.CompilerParams(
        dimension_semantics=("arbitrary",) * n_axes,
        vmem_limit_bytes=VMEM_LIMIT_BYTES,
    )


def _resident(shape):
    zeros = (0,) * len(shape)
    return pl.BlockSpec(shape, lambda *_: zeros, pipeline_mode=pl.Buffered(1))


def _resident_layer(stacked, layer):
    shape = stacked.shape[1:]
    index = (layer,) + (0,) * len(shape)
    return pl.BlockSpec((None,) + shape, lambda *_: index, pipeline_mode=pl.Buffered(1))


def _rmsnorm(x, g):
    return x * lax.rsqrt(jnp.mean(x * x, axis=-1, keepdims=True) + EPS) * g


def _norm_modulate(x, g, shift, scale):
    return _rmsnorm(x, g) * (1.0 + scale) + shift


def _causal_conv_rows(u, prev8, taps, bias):
    top = u[0:8]
    row8 = lax.broadcasted_iota(jnp.int32, top.shape, 0)
    y = taps[0] * u + bias
    y_top = taps[0] * top + bias
    for j in range(1, len(taps)):
        y = y + taps[j] * pltpu.roll(u, j, 0)
        shifted_top = jnp.where(row8 < j, pltpu.roll(prev8, j, 0), pltpu.roll(top, j, 0))
        y_top = y_top + taps[j] * shifted_top
    return jnp.concatenate([y_top, y[8:]], axis=0)


def _mods_kernel(c_ref, w_ref, b_ref, o_ref):
    c = c_ref[...]
    c_act = (c * jax.nn.sigmoid(c)).astype(BF16)
    o_ref[0] = jnp.dot(c_act, w_ref[0].astype(BF16), preferred_element_type=F32) + b_ref[0]


def _mods_call(c, ada_w, ada_b):
    depth, d, n = ada_w.shape
    b = c.shape[0]
    tn = 1024
    return pl.pallas_call(
        _mods_kernel,
        grid=(depth, n // tn),
        in_specs=[
            pl.BlockSpec((b, d), lambda l, j: (0, 0)),
            pl.BlockSpec((1, d, tn), lambda l, j: (l, 0, j)),
            pl.BlockSpec((1, 1, tn), lambda l, j: (l, 0, j)),
        ],
        out_specs=pl.BlockSpec((1, b, tn), lambda l, j: (l, 0, j)),
        out_shape=jax.ShapeDtypeStruct((depth, b, n), F32),
        compiler_params=_params(2),
        name="adaln_mods",
    )(c, ada_w, ada_b.reshape(depth, 1, n))


def _ffn_kernel(x_ref, mod_ref, ng_ref, wup_ref, cw_ref, cb_ref, wdn_ref, *rest, tm, f_dim, fc,
                final):
    if final:
        fg_ref, o_ref, h_scr, act_scr, halo_scr = rest
    else:
        o_ref, h_scr, act_scr, halo_scr = rest

    @pl.when(pl.program_id(1) == 0)
    def _():
        halo_scr[...] = jnp.zeros_like(halo_scr)

    x = x_ref[0]
    shift, scale, gate = mod_ref[0, 3:4, :], mod_ref[0, 4:5, :], mod_ref[0, 5:6, :]
    h_scr[...] = _norm_modulate(x, ng_ref[...], shift, scale).astype(BF16)

    for f in range(f_dim // fc):
        halves = []
        for half in range(2):
            c0 = half * f_dim + f * fc
            cols = slice(c0, c0 + fc)
            u = jnp.dot(h_scr[...], wup_ref[:, cols], preferred_element_type=F32)
            prev8 = halo_scr[:, cols]
            halo_scr[:, cols] = u[tm - 8:tm]
            taps = (cw_ref[2:3, cols], cw_ref[1:2, cols], cw_ref[0:1, cols])
            halves.append(_causal_conv_rows(u, prev8, taps, cb_ref[0:1, cols]))
        yg, yv = halves
        act_scr[:, f * fc:(f + 1) * fc] = (yg * jax.nn.sigmoid(yg) * yv).astype(BF16)

    y = jnp.dot(act_scr[...], wdn_ref[...], preferred_element_type=F32)
    out = x + gate * y
    if final:
        out = _rmsnorm(out, fg_ref[...])
    o_ref[0] = out


def _ffn_call(x, mod, norm_g, w_up_all, conv_w, conv_b, w_down_all, layer, final_g=None,
              tm=TM_FFN, fc=256):
    b, s, d = x.shape
    f_dim = w_down_all.shape[1]
    assert s % tm == 0 and f_dim % fc == 0
    final = final_g is not None
    row_spec = pl.BlockSpec((1, tm, d), lambda i, j: (i, j, 0))
    in_specs = [
        row_spec,
        pl.BlockSpec((1, 6, d), lambda i, j: (i, 0, 0)),
        _resident((1, d)),
        _resident_layer(w_up_all, layer),
        _resident((3, 2 * f_dim)),
        _resident((1, 2 * f_dim)),
        _resident_layer(w_down_all, layer),
    ]
    args = [x, mod, norm_g.reshape(1, d), w_up_all, conv_w, conv_b.reshape(1, -1), w_down_all]
    if final:
        in_specs.append(_resident((1, d)))
        args.append(final_g.reshape(1, d))
    return pl.pallas_call(
        functools.partial(_ffn_kernel, tm=tm, f_dim=f_dim, fc=fc, final=final),
        grid=(b, s // tm),
        in_specs=in_specs,
        out_specs=row_spec,
        out_shape=jax.ShapeDtypeStruct((b, s, d), F32),
        scratch_shapes=[
            pltpu.VMEM((tm, d), BF16),
            pltpu.VMEM((tm, f_dim), BF16),
            pltpu.VMEM((8, 2 * f_dim), F32),
        ],
        compiler_params=_params(2),
        name="conv_ffn",
    )(*args)


CONF_PAD = 32
CONF_LANE_PAD = 128


def _conf_kernel(x_ref, mod_ref, ng_ref, w1_ref, b1_ref, dw_ref, db_ref, lg_ref, lb_ref, w2_ref,
                 b2_ref, o_ref, stage_scr, y_scr, taps_scr, *, tm, d, kw, rb, cw):
    n = tm + CONF_PAD
    ext_scr = stage_scr.at[0]

    @pl.when(pl.program_id(1) == 0)
    def _():
        ext_scr[0:CONF_PAD, 0:d] = jnp.zeros((CONF_PAD, d), F32)

    @pl.when((pl.program_id(0) == 0) & (pl.program_id(1) == 0))
    def _():
        for k in range(kw):
            taps_scr[k] = jnp.broadcast_to(dw_ref[k:k + 1, :], (8, d))

    x = x_ref[0]
    shift, scale, gate = mod_ref[0, 0:1, :], mod_ref[0, 1:2, :], mod_ref[0, 2:3, :]
    h = _norm_modulate(x, ng_ref[...], shift, scale).astype(BF16)
    a = jnp.dot(h, w1_ref[:, 0:d], preferred_element_type=F32) + b1_ref[0:1, 0:d]
    g = jnp.dot(h, w1_ref[:, d:2 * d], preferred_element_type=F32) + b1_ref[0:1, d:2 * d]
    ext_scr[CONF_PAD:CONF_PAD + tm, 0:d] = a * jax.nn.sigmoid(g)

    ext = ext_scr[0:n, 0:d]
    for b in range(1, 8):
        stage_scr[b, 0:n - 8, 0:d] = pltpu.roll(ext, n - b, 0)[0:n - 8]

    base = CONF_PAD - (kw - 1)
    for c in range(d // cw):
        cols = slice(c * cw, (c + 1) * cw)
        taps = [jnp.concatenate([taps_scr[k, :, cols]] * (rb // 8), axis=0) for k in range(kw)]
        bias = jnp.broadcast_to(db_ref[0:1, cols], (rb, cw))

        def row_block(i, carry, cols=cols, taps=taps, bias=bias):
            r0 = pl.multiple_of(i * rb, rb)
            acc = bias
            for k in range(kw):
                q8, b = divmod(base + k, 8)
                acc = acc + taps[k] * stage_scr[b, pl.ds(r0 + 8 * q8, rb), cols]
            y_scr[pl.ds(r0, rb), cols] = acc
            return carry

        lax.fori_loop(0, tm // rb, row_block, 0)
    ext_scr[0:CONF_PAD, 0:d] = ext_scr[tm:tm + CONF_PAD, 0:d]

    y = y_scr[...]
    mu = jnp.mean(y, axis=-1, keepdims=True)
    var = jnp.mean(jnp.square(y - mu), axis=-1, keepdims=True)
    ln = (y - mu) * lax.rsqrt(var + EPS) * lg_ref[...] + lb_ref[...]
    act = (ln * jax.nn.sigmoid(ln)).astype(BF16)
    out = jnp.dot(act, w2_ref[...], preferred_element_type=F32) + b2_ref[...]
    o_ref[0] = x + gate * out


def _conf_call(x, mod, norm_g, w_pw1_all, b_pw1, dw_w, dw_b, ln_g, ln_b, w_pw2_all, b_pw2, layer,
               stage_rows_pad=0):
    b, s, d = x.shape
    kw = dw_w.shape[0]
    tm, rb, cw = TM_CONF, 128, 128
    assert s % tm == 0 and kw - 1 <= CONF_PAD
    row_spec = pl.BlockSpec((1, tm, d), lambda i, j: (i, j, 0))
    n = tm + CONF_PAD
    return pl.pallas_call(
        functools.partial(_conf_kernel, tm=tm, d=d, kw=kw, rb=rb, cw=cw),
        grid=(b, s // tm),
        in_specs=[
            row_spec,
            pl.BlockSpec((1, 6, d), lambda i, j: (i, 0, 0)),
            _resident((1, d)),
            _resident_layer(w_pw1_all, layer),
            _resident((1, 2 * d)),
            _resident((kw, d)),
            _resident((1, d)),
            _resident((1, d)),
            _resident((1, d)),
            _resident_layer(w_pw2_all, layer),
            _resident((1, d)),
        ],
        out_specs=row_spec,
        out_shape=jax.ShapeDtypeStruct((b, s, d), F32),
        scratch_shapes=[
            pltpu.VMEM((8, n + stage_rows_pad, d + CONF_LANE_PAD), F32),
            pltpu.VMEM((tm, d), F32),
            pltpu.VMEM((kw, 8, d), F32),
        ],
        compiler_params=_params(2),
        name="conformer_conv",
    )(x, mod, norm_g.reshape(1, d), w_pw1_all, b_pw1.reshape(1, -1), dw_w, dw_b.reshape(1, d),
      ln_g.reshape(1, d), ln_b.reshape(1, d), w_pw2_all, b_pw2.reshape(1, d))


def _split3(x):
    hi = x.astype(BF16)
    rest = x - hi.astype(F32)
    mid = rest.astype(BF16)
    lo = (rest - mid.astype(F32)).astype(BF16)
    return hi, mid, lo


def _gate_factor_constants(tm, seg):
    gl, nh, n = GATE_GROUP, MLSTM_HEADS, GATE_LANES
    col_place = np.zeros((3, n, n), np.float32)
    row_place = np.zeros((3, n, n), np.float32)
    col_ones = np.zeros((1, n), np.float32)
    row_ones = np.zeros((n, n), np.float32)
    expand = np.zeros((n, 2 * n), np.float32)
    for h in range(nh):
        for j in range(3):
            col_place[j, h, gl * h + j] = 1.0
            col_place[j, nh + h, gl * h + 9 + j] = 1.0
            row_place[j, gl * h + 3 + j, nh + h] = 1.0
            row_place[j, gl * h + 6 + j, h] = -1.0
            col_ones[0, gl * h + 3 + j] = 1.0
            col_ones[0, gl * h + 6 + j] = 1.0
            row_ones[gl * h + j, :] = 1.0
            expand[gl * h + j, :n] = 1.0
            expand[gl * h + j, n:] = -1.0
            expand[gl * h + 9 + j, n:] = 1.0
    pos = np.arange(tm)
    tri = ((pos[:, None] >= pos[None, :]) & (pos[:, None] // seg == pos[None, :] // seg))
    as_bf16 = lambda a: jnp.asarray(a, BF16)
    return (as_bf16(col_place), as_bf16(row_place), jnp.asarray(col_ones), jnp.asarray(row_ones),
            as_bf16(expand), as_bf16(tri.astype(np.float32)))


def _mfront_kernel(x_ref, mod_ref, ng_ref, wup_ref, cw_ref, cb_ref, wq_ref, wkt_ref, wv_ref,
                   wgc_ref, wgm_ref, bif_ref, cplace_ref, rplace_ref, cones_ref, rones_ref, tri_ref,
                   q_ref, kt_ref, v_ref, xc_ref, z_ref, gcol_ref, grow_ref,
                   h_scr, halo_scr, *, tm, inner, nc, k_scale):
    n_heads = MLSTM_HEADS

    @pl.when(pl.program_id(1) == 0)
    def _():
        halo_scr[...] = jnp.zeros_like(halo_scr)

    x = x_ref[0]
    shift, scale = mod_ref[0, 0:1, :], mod_ref[0, 1:2, :]
    h_scr[...] = _norm_modulate(x, ng_ref[...], shift, scale).astype(BF16)

    kw = cw_ref.shape[0]
    gates = jnp.broadcast_to(bif_ref[...], (tm, GATE_LANES))
    n_chunks = inner // nc
    xm_next = jnp.dot(h_scr[...], wup_ref[:, 0:nc], preferred_element_type=F32)
    for c in range(n_chunks):
        cols = slice(c * nc, (c + 1) * nc)
        xm = xm_next
        if c + 1 < n_chunks:
            xm_next = jnp.dot(h_scr[...], wup_ref[:, (c + 1) * nc:(c + 2) * nc],
                              preferred_element_type=F32)
        prev8 = halo_scr[:, cols]
        halo_scr[:, cols] = xm[tm - 8:tm]
        taps = tuple(cw_ref[kw - 1 - j:kw - j, cols] for j in range(kw))
        y = _causal_conv_rows(xm, prev8, taps, cb_ref[0:1, cols])
        xm_b = xm.astype(BF16)
        xc_b = (y * jax.nn.sigmoid(y)).astype(BF16)
        xc_ref[0, :, cols] = xc_b

        for g in range(c * nc // BD_GROUP, (c + 1) * nc // BD_GROUP):
            gcols = slice(g * BD_GROUP, (g + 1) * BD_GROUP)
            sub = slice(g * BD_GROUP - c * nc, (g + 1) * BD_GROUP - c * nc)
            xc_g, xm_g = xc_b[:, sub], xm_b[:, sub]
            q = jnp.dot(xc_g, wq_ref[g], preferred_element_type=F32)
            v = jnp.dot(xm_g, wv_ref[g], preferred_element_type=F32)
            k_t = lax.dot_general(wkt_ref[g], xc_g, (((1,), (1,)), ((), ())),
                                  preferred_element_type=F32)
            gates = gates + jnp.dot(xc_g, wgc_ref[g], preferred_element_type=F32)
            gates = gates + jnp.dot(xm_g, wgm_ref[g], preferred_element_type=F32)
            q_ref[0, :, gcols] = q.astype(BF16)
            kt_ref[0, gcols, :] = (k_t * k_scale).astype(BF16)
            v_ref[0, :, gcols] = v.astype(BF16)

        zcols = slice(inner + c * nc, inner + (c + 1) * nc)
        z_ref[0, :, cols] = jnp.dot(h_scr[...], wup_ref[:, zcols],
                                    preferred_element_type=F32).astype(BF16)

    lane = lax.broadcasted_iota(jnp.int32, gates.shape, 1)
    log_f = jax.nn.log_sigmoid(gates)
    csum = jnp.zeros((tm, GATE_LANES), F32)
    for piece in _split3(log_f):
        csum = csum + jnp.dot(tri_ref[...], piece, preferred_element_type=F32)
    gi = jnp.where(lane < n_heads, csum, gates)
    gcol = jnp.broadcast_to(cones_ref[...], (tm, GATE_LANES))
    grow = jnp.concatenate([rones_ref[...]] * (tm // GATE_LANES), axis=1)
    for j, piece in enumerate(_split3(gi)):
        gcol = gcol + jnp.dot(piece, cplace_ref[j], preferred_element_type=F32)
        grow = grow + lax.dot_general(rplace_ref[j], piece, (((1,), (1,)), ((), ())),
                                      preferred_element_type=F32)
    gcol_ref[0] = gcol.astype(BF16)
    grow_ref[0] = grow.astype(BF16)


def _qkv_prep_kernel(wq_ref, wk_ref, wkt_ref, wv_ref, wif_ref, dq_ref, dkt_ref, dv_ref, gc_ref,
                     gm_ref):
    row = lax.broadcasted_iota(jnp.int32, (BD_GROUP, BD_GROUP), 0)
    col = lax.broadcasted_iota(jnp.int32, (BD_GROUP, BD_GROUP), 1)
    same_block = (row // QKV_BLOCK) == (col // QKV_BLOCK)

    def dense(w_ref):
        w = w_ref[...]
        tiled = jnp.zeros((BD_GROUP, BD_GROUP), F32)
        for o in range(QKV_BLOCK):
            tiled = jnp.where(col % QKV_BLOCK == o, w[:, o:o + 1], tiled)
        return jnp.where(same_block, tiled, 0.0)

    dq, dk, dv = dense(wq_ref), dense(wk_ref), dense(wv_ref)
    dq_ref[...] = dq.astype(BF16)
    dkt_ref[...] = dense(wkt_ref).astype(BF16)
    dv_ref[...] = dv.astype(BF16)
    hi = lax.Precision.HIGHEST
    gc = (jnp.dot(dq, wif_ref[0], preferred_element_type=F32, precision=hi)
          + jnp.dot(dk, wif_ref[1], preferred_element_type=F32, precision=hi))
    gc_ref[...] = gc.astype(BF16)
    gm_ref[...] = jnp.dot(dv, wif_ref[2], preferred_element_type=F32, precision=hi).astype(BF16)


def _qkv_prep_call(wq, wk, wv, w_if):
    n_layers, n_blocks = wq.shape[0], wq.shape[1]
    n_heads = MLSTM_HEADS
    inner = n_blocks * QKV_BLOCK
    n_groups = inner // BD_GROUP
    rows = lambda w: w.reshape(n_layers, n_groups, BD_GROUP, QKV_BLOCK)
    w_if4 = w_if.reshape(n_layers, 3, inner, 2 * n_heads)
    w_if4 = jnp.concatenate([w_if4[..., n_heads:], w_if4[..., :n_heads]], axis=-1)
    w_if4 = jnp.pad(w_if4, ((0, 0), (0, 0), (0, 0), (0, GATE_LANES - 2 * n_heads)))
    w_spec = pl.BlockSpec((None, None, BD_GROUP, QKV_BLOCK), lambda l, g: (l, g, 0, 0))
    dense_spec = pl.BlockSpec((None, None, BD_GROUP, BD_GROUP), lambda l, g: (l, g, 0, 0))
    gate_spec = pl.BlockSpec((None, None, BD_GROUP, GATE_LANES), lambda l, g: (l, g, 0, 0))
    dense_shape = jax.ShapeDtypeStruct((n_layers, n_groups, BD_GROUP, BD_GROUP), BF16)
    gate_shape = jax.ShapeDtypeStruct((n_layers, n_groups, BD_GROUP, GATE_LANES), BF16)
    return pl.pallas_call(
        _qkv_prep_kernel,
        grid=(n_layers, n_groups),
        in_specs=[w_spec, w_spec, w_spec, w_spec,
                  pl.BlockSpec((None, 3, BD_GROUP, GATE_LANES), lambda l, g: (l, 0, g, 0))],
        out_specs=[dense_spec, dense_spec, dense_spec, gate_spec, gate_spec],
        out_shape=[dense_shape, dense_shape, dense_shape, gate_shape, gate_shape],
        compiler_params=_params(2),
        name="qkv_weight_prep",
    )(rows(wq), rows(wk), rows(jnp.swapaxes(wk, -1, -2)), rows(wv), w_if4)


def _mfront_call(x, mod, norm_g, w_up_all, conv_w, conv_b, qkv_weights, b_if, layer, tm=TM_MFRONT):
    b, s, d = x.shape
    inner = w_up_all.shape[2] // 2
    n_heads = MLSTM_HEADS
    assert s % tm == 0 and tm % CHUNK_LEN == 0 and 2 * n_heads == 8
    dh = inner // n_heads
    b_if2 = jnp.concatenate([b_if[n_heads:], b_if[:n_heads]])
    b_if2 = jnp.pad(b_if2, (0, GATE_LANES - 2 * n_heads)).reshape(1, GATE_LANES)
    col_place, row_place, col_ones, row_ones, _, tri = _gate_factor_constants(tm, CHUNK_LEN)

    row_spec = pl.BlockSpec((1, tm, d), lambda i, j: (i, j, 0))
    wide_spec = pl.BlockSpec((1, tm, inner), lambda i, j: (i, j, 0))
    wide_shape = jax.ShapeDtypeStruct((b, s, inner), BF16)
    return pl.pallas_call(
        functools.partial(_mfront_kernel, tm=tm, inner=inner, nc=512, k_scale=float(dh) ** -0.5),
        grid=(b, s // tm),
        in_specs=[
            row_spec,
            pl.BlockSpec((1, 6, d), lambda i, j: (i, 0, 0)),
            _resident((1, d)),
            _resident_layer(w_up_all, layer),
            _resident(conv_w.shape),
            _resident((1, inner)),
            *[_resident_layer(w, layer) for w in qkv_weights],
            _resident((1, GATE_LANES)),
            _resident((3, GATE_LANES, GATE_LANES)),
            _resident((3, GATE_LANES, GATE_LANES)),
            _resident((1, GATE_LANES)),
            _resident((GATE_LANES, GATE_LANES)),
            _resident((tm, tm)),
        ],
        out_specs=[
            wide_spec,
            pl.BlockSpec((1, inner, tm), lambda i, j: (i, 0, j)),
            wide_spec, wide_spec, wide_spec,
            pl.BlockSpec((1, tm, GATE_LANES), lambda i, j: (i, j, 0)),
            pl.BlockSpec((1, GATE_LANES, tm), lambda i, j: (i, 0, j)),
        ],
        out_shape=[
            wide_shape,
            jax.ShapeDtypeStruct((b, inner, s), BF16),
            wide_shape, wide_shape, wide_shape,
            jax.ShapeDtypeStruct((b, s, GATE_LANES), BF16),
            jax.ShapeDtypeStruct((b, GATE_LANES, s), BF16),
        ],
        scratch_shapes=[
            pltpu.VMEM((tm, d), BF16),
            pltpu.VMEM((8, inner), F32),
        ],
        compiler_params=_params(2),
        name="mlstm_front",
    )(x, mod, norm_g.reshape(1, d), w_up_all, conv_w, conv_b.reshape(1, inner), *qkv_weights,
      b_if2, col_place, row_place, col_ones, row_ones, tri)


def _lanes(a, n):
    return jnp.concatenate([a] * n, axis=1)


def _mchunk_kernel(q_ref, kt_ref, v_ref, xc_ref, z_ref, gcol_ref, grow_ref, expand_ref,
                   x_ref, mod_ref, lnw_ref, skip_ref, wdn_ref, o_ref, c_scr, m_scr, hn_scr, *,
                   ln, dh):
    n_heads = MLSTM_HEADS
    rep = GATE_LANES
    n_rep = dh // rep

    @pl.when(pl.program_id(1) == 0)
    def _():
        c_scr[...] = jnp.zeros_like(c_scr)
        m_scr[...] = jnp.zeros_like(m_scr)

    causal = (lax.broadcasted_iota(jnp.int32, (ln, ln), 0)
              >= lax.broadcasted_iota(jnp.int32, (ln, ln), 1))
    group = lax.broadcasted_iota(jnp.int32, (ln, rep), 1) // GATE_GROUP
    gcol = gcol_ref[0]
    for h in range(n_heads):
        cols = slice(h * dh, (h + 1) * dh)
        q, k_t, v = q_ref[0, :, cols], kt_ref[0, cols, :], v_ref[0, :, cols]
        gcol_h = jnp.where(group == h, gcol, jnp.zeros_like(gcol))
        dmat = jnp.dot(gcol_h, grow_ref[0], preferred_element_type=F32)
        gi = jnp.dot(gcol_h, expand_ref[...], preferred_element_type=F32)
        g, i_minus_g = gi[:, 0:rep], gi[:, rep:2 * rep]
        m = m_scr[h, 0:1, :]
        g_tot = g[ln - 1:ln, :]

        dmat = jnp.where(causal, dmat, -jnp.inf)
        inter = m + g
        m_row = jnp.maximum(inter, jnp.max(dmat, axis=-1, keepdims=True))
        w_intra = jnp.exp(dmat - _lanes(m_row, ln // rep))
        w_inter = jnp.exp(inter - m_row)
        s = jnp.dot(q, k_t, preferred_element_type=F32) * w_intra
        c_t = c_scr[h]
        qc = jnp.dot(q, c_t.astype(BF16), preferred_element_type=F32)
        num = (_lanes(w_inter, n_rep) * qc[:, 0:dh]
               + jnp.dot(s.astype(BF16), v, preferred_element_type=F32))
        den = w_inter * qc[:, dh:dh + rep] + jnp.sum(s, axis=-1, keepdims=True)
        inv = 1.0 / jnp.maximum(jnp.abs(den), jnp.exp(-m_row))
        hh = num * _lanes(inv, n_rep)
        mu = jnp.mean(hh, axis=-1, keepdims=True)
        var = jnp.mean(jnp.square(hh - mu), axis=-1, keepdims=True)
        hn_scr[:, cols] = (hh - mu) * lax.rsqrt(var + EPS)

        decay = i_minus_g + g_tot
        m_new = jnp.maximum(m + g_tot, jnp.max(decay, axis=0, keepdims=True))
        w_s = jnp.exp(decay - m_new)
        w_c = jnp.exp(m + g_tot - m_new)
        vw = jnp.concatenate(
            [(_lanes(w_s, n_rep) * v.astype(F32)).astype(BF16), w_s.astype(BF16)], axis=1)
        d_c = jnp.dot(k_t, vw, preferred_element_type=F32)
        c_scr[h] = _lanes(w_c, n_rep + 1) * c_t + d_c
        m_scr[h] = jnp.broadcast_to(m_new, m_scr.shape[1:])

    z = z_ref[0].astype(F32)
    pre = ((hn_scr[...] * lnw_ref[...] + skip_ref[...] * xc_ref[0].astype(F32))
           * (z * jax.nn.sigmoid(z)))
    y = jnp.dot(pre.astype(BF16), wdn_ref[...], preferred_element_type=F32)
    o_ref[0] = x_ref[0] + mod_ref[0, 2:3, :] * y


def _mchunk_call(q, k_t, v, xc, z, gcol, grow, x, mod, ln_w, skip, w_down_all, layer):
    b, s, d = x.shape
    inner = q.shape[-1]
    n_heads = MLSTM_HEADS
    dh = inner // n_heads
    ln = CHUNK_LEN
    wide_spec = pl.BlockSpec((1, ln, inner), lambda i, j: (i, j, 0))
    row_spec = pl.BlockSpec((1, ln, d), lambda i, j: (i, j, 0))
    return pl.pallas_call(
        functools.partial(_mchunk_kernel, ln=ln, dh=dh),
        grid=(b, s // ln),
        in_specs=[
            wide_spec,
            pl.BlockSpec((1, inner, ln), lambda i, j: (i, 0, j)),
            wide_spec, wide_spec, wide_spec,
            pl.BlockSpec((1, ln, GATE_LANES), lambda i, j: (i, j, 0)),
            pl.BlockSpec((1, GATE_LANES, ln), lambda i, j: (i, 0, j)),
            _resident((GATE_LANES, 2 * GATE_LANES)),
            row_spec,
            pl.BlockSpec((1, 6, d), lambda i, j: (i, 0, 0)),
            _resident((1, inner)),
            _resident((1, inner)),
            _resident_layer(w_down_all, layer),
        ],
        out_specs=row_spec,
        out_shape=jax.ShapeDtypeStruct((b, s, d), F32),
        scratch_shapes=[
            pltpu.VMEM((n_heads, dh, dh + GATE_LANES), F32),
            pltpu.VMEM((n_heads, 8, GATE_LANES), F32),
            pltpu.VMEM((ln, inner), F32),
        ],
        compiler_params=_params(2),
        name="mlstm_chunk",
    )(q, k_t, v, xc, z, gcol, grow, _gate_factor_constants(ln, ln)[4], x, mod,
      ln_w.reshape(1, inner), skip.reshape(1, inner), w_down_all)


def kernel(x, c, ada_w, ada_b, norm_mix_g, norm_ffn_g, final_g, mlstm_w_up, mlstm_conv_w,
           mlstm_conv_b, mlstm_wq, mlstm_wk, mlstm_wv, mlstm_w_if, mlstm_b_if, mlstm_ln_w,
           mlstm_skip, mlstm_w_down, conf_w_pw1, conf_b_pw1, conf_dw_w, conf_dw_b, conf_ln_g,
           conf_ln_b, conf_w_pw2, conf_b_pw2, ffn_w_up, ffn_conv_w, ffn_conv_b, ffn_w_down):
    depth = ada_w.shape[0]
    b, _, d = x.shape
    mods = _mods_call(c, ada_w, ada_b).reshape(depth, b, 6, d)
    mlstm_w_up, mlstm_w_down = mlstm_w_up.astype(BF16), mlstm_w_down.astype(BF16)
    conf_w_pw1, conf_w_pw2 = conf_w_pw1.astype(BF16), conf_w_pw2.astype(BF16)
    ffn_w_up, ffn_w_down = ffn_w_up.astype(BF16), ffn_w_down.astype(BF16)
    qkv_weights = _qkv_prep_call(mlstm_wq, mlstm_wk, mlstm_wv, mlstm_w_if)
    for layer in range(depth):
        mod = mods[layer]
        j = layer // 2
        if layer % 2 == 0:
            q, k_t, v, xc, z, gcol, grow = _mfront_call(
                x, mod, norm_mix_g[layer], mlstm_w_up, mlstm_conv_w[j], mlstm_conv_b[j],
                qkv_weights, mlstm_b_if[j], j)
            x = _mchunk_call(q, k_t, v, xc, z, gcol, grow, x, mod, mlstm_ln_w[j], mlstm_skip[j],
                             mlstm_w_down, j)
        else:
            x = _conf_call(x, mod, norm_mix_g[layer], conf_w_pw1, conf_b_pw1[j], conf_dw_w[j],
                           conf_dw_b[j], conf_ln_g[j], conf_ln_b[j], conf_w_pw2, conf_b_pw2[j], j,
                           stage_rows_pad=0 if j == 0 else 8)
        x = _ffn_call(x, mod, norm_ffn_g[layer], ffn_w_up, ffn_conv_w[layer], ffn_conv_b[layer],
                      ffn_w_down, layer, final_g if layer == depth - 1 else None)
    return x
```

```python
import functools

import jax
import jax.numpy as jnp
import numpy as np
from jax import lax
from jax.experimental import pallas as pl
from jax.experimental.pallas import tpu as pltpu

F32 = jnp.float32
BF16 = jnp.bfloat16
EPS = 1e-6

MLSTM_HEADS = 4
QKV_BLOCK = 4
GATE_LANES = 128
GATE_GROUP = 16
BD_GROUP = 256

TM_FFN = 1024
TM_CONF = 512
TM_MFRONT = 512
CHUNK_LEN = 256

VMEM_LIMIT_BYTES = 56 * 1024 * 1024


def _params(n_axes):
    return pltpu.CompilerParams(
        dimension_semantics=("arbitrary",) * n_axes,
        vmem_limit_bytes=VMEM_LIMIT_BYTES,
    )


def _resident(shape):
    zeros = (0,) * len(shape)
    return pl.BlockSpec(shape, lambda *_: zeros, pipeline_mode=pl.Buffered(1))


def _resident_layer(stacked, layer):
    shape = stacked.shape[1:]
    index = (layer,) + (0,) * len(shape)
    return pl.BlockSpec((None,) + shape, lambda *_: index, pipeline_mode=pl.Buffered(1))


def _rmsnorm(x, g):
    return x * lax.rsqrt(jnp.mean(x * x, axis=-1, keepdims=True) + EPS) * g


def _norm_modulate(x, g, shift, scale):
    return _rmsnorm(x, g) * (1.0 + scale) + shift


def _causal_conv_rows(u, prev8, taps, bias):
    top = u[0:8]
    row8 = lax.broadcasted_iota(jnp.int32, top.shape, 0)
    y = taps[0] * u + bias
    y_top = taps[0] * top + bias
    for j in range(1, len(taps)):
        y = y + taps[j] * pltpu.roll(u, j, 0)
        shifted_top = jnp.where(row8 < j, pltpu.roll(prev8, j, 0), pltpu.roll(top, j, 0))
        y_top = y_top + taps[j] * shifted_top
    return jnp.concatenate([y_top, y[8:]], axis=0)


def _mods_kernel(c_ref, w_ref, b_ref, o_ref):
    c = c_ref[...]
    c_act = (c * jax.nn.sigmoid(c)).astype(BF16)
    o_ref[0] = jnp.dot(c_act, w_ref[0].astype(BF16), preferred_element_type=F32) + b_ref[0]


def _mods_call(c, ada_w, ada_b):
    depth, d, n = ada_w.shape
    b = c.shape[0]
    tn = 1024
    return pl.pallas_call(
        _mods_kernel,
        grid=(depth, n // tn),
        in_specs=[
            pl.BlockSpec((b, d), lambda l, j: (0, 0)),
            pl.BlockSpec((1, d, tn), lambda l, j: (l, 0, j)),
            pl.BlockSpec((1, 1, tn), lambda l, j: (l, 0, j)),
        ],
        out_specs=pl.BlockSpec((1, b, tn), lambda l, j: (l, 0, j)),
        out_shape=jax.ShapeDtypeStruct((depth, b, n), F32),
        compiler_params=_params(2),
        name="adaln_mods",
    )(c, ada_w, ada_b.reshape(depth, 1, n))


def _ffn_kernel(x_ref, mod_ref, ng_ref, wup_ref, cw_ref, cb_ref, wdn_ref, *rest, tm, f_dim, fc,
                final):
    if final:
        fg_ref, o_ref, h_scr, act_scr, halo_scr = rest
    else:
        o_ref, h_scr, act_scr, halo_scr = rest

    @pl.when(pl.program_id(1) == 0)
    def _():
        halo_scr[...] = jnp.zeros_like(halo_scr)

    x = x_ref[0]
    shift, scale, gate = mod_ref[0, 3:4, :], mod_ref[0, 4:5, :], mod_ref[0, 5:6, :]
    h_scr[...] = _norm_modulate(x, ng_ref[...], shift, scale).astype(BF16)

    for f in range(f_dim // fc):
        halves = []
        for half in range(2):
            c0 = half * f_dim + f * fc
            cols = slice(c0, c0 + fc)
            u = jnp.dot(h_scr[...], wup_ref[:, cols], preferred_element_type=F32)
            prev8 = halo_scr[:, cols]
            halo_scr[:, cols] = u[tm - 8:tm]
            taps = (cw_ref[2:3, cols], cw_ref[1:2, cols], cw_ref[0:1, cols])
            halves.append(_causal_conv_rows(u, prev8, taps, cb_ref[0:1, cols]))
        yg, yv = halves
        act_scr[:, f * fc:(f + 1) * fc] = (yg * jax.nn.sigmoid(yg) * yv).astype(BF16)

    y = jnp.dot(act_scr[...], wdn_ref[...], preferred_element_type=F32)
    out = x + gate * y
    if final:
        out = _rmsnorm(out, fg_ref[...])
    o_ref[0] = out


def _ffn_call(x, mod, norm_g, w_up_all, conv_w, conv_b, w_down_all, layer, final_g=None,
              tm=TM_FFN, fc=256):
    b, s, d = x.shape
    f_dim = w_down_all.shape[1]
    assert s % tm == 0 and f_dim % fc == 0
    final = final_g is not None
    row_spec = pl.BlockSpec((1, tm, d), lambda i, j: (i, j, 0))
    in_specs = [
        row_spec,
        pl.BlockSpec((1, 6, d), lambda i, j: (i, 0, 0)),
        _resident((1, d)),
        _resident_layer(w_up_all, layer),
        _resident((3, 2 * f_dim)),
        _resident((1, 2 * f_dim)),
        _resident_layer(w_down_all, layer),
    ]
    args = [x, mod, norm_g.reshape(1, d), w_up_all, conv_w, conv_b.reshape(1, -1), w_down_all]
    if final:
        in_specs.append(_resident((1, d)))
        args.append(final_g.reshape(1, d))
    return pl.pallas_call(
        functools.partial(_ffn_kernel, tm=tm, f_dim=f_dim, fc=fc, final=final),
        grid=(b, s // tm),
        in_specs=in_specs,
        out_specs=row_spec,
        out_shape=jax.ShapeDtypeStruct((b, s, d), F32),
        scratch_shapes=[
            pltpu.VMEM((tm, d), BF16),
            pltpu.VMEM((tm, f_dim), BF16),
            pltpu.VMEM((8, 2 * f_dim), F32),
        ],
        compiler_params=_params(2),
        name="conv_ffn",
    )(*args)


CONF_PAD = 32
CONF_LANE_PAD = 128
CONF_ROW_PAD = 8


def _conf_kernel(x_ref, mod_ref, ng_ref, w1_ref, b1_ref, dw_ref, db_ref, lg_ref, lb_ref, w2_ref,
                 b2_ref, o_ref, stage_scr, y_scr, taps_scr, *, tm, d, kw, rb, cw):
    n = tm + CONF_PAD
    ext_scr = stage_scr.at[0]

    @pl.when(pl.program_id(1) == 0)
    def _():
        ext_scr[0:CONF_PAD, 0:d] = jnp.zeros((CONF_PAD, d), F32)

    @pl.when((pl.program_id(0) == 0) & (pl.program_id(1) == 0))
    def _():
        for k in range(kw):
            taps_scr[k] = jnp.broadcast_to(dw_ref[k:k + 1, :], (8, d))

    x = x_ref[0]
    shift, scale, gate = mod_ref[0, 0:1, :], mod_ref[0, 1:2, :], mod_ref[0, 2:3, :]
    h = _norm_modulate(x, ng_ref[...], shift, scale).astype(BF16)
    a = jnp.dot(h, w1_ref[:, 0:d], preferred_element_type=F32) + b1_ref[0:1, 0:d]
    g = jnp.dot(h, w1_ref[:, d:2 * d], preferred_element_type=F32) + b1_ref[0:1, d:2 * d]
    ext_scr[CONF_PAD:CONF_PAD + tm, 0:d] = a * jax.nn.sigmoid(g)

    ext = ext_scr[0:n, 0:d]
    for b in range(1, 8):
        stage_scr[b, 0:n - 8, 0:d] = pltpu.roll(ext, n - b, 0)[0:n - 8]

    base = CONF_PAD - (kw - 1)
    for c in range(d // cw):
        cols = slice(c * cw, (c + 1) * cw)
        taps = [jnp.concatenate([taps_scr[k, :, cols]] * (rb // 8), axis=0) for k in range(kw)]
        bias = jnp.broadcast_to(db_ref[0:1, cols], (rb, cw))

        def row_block(i, carry, cols=cols, taps=taps, bias=bias):
            r0 = pl.multiple_of(i * rb, rb)
            acc = bias
            for k in range(kw):
                q8, b = divmod(base + k, 8)
                acc = acc + taps[k] * stage_scr[b, pl.ds(r0 + 8 * q8, rb), cols]
            y_scr[pl.ds(r0, rb), cols] = acc
            return carry

        lax.fori_loop(0, tm // rb, row_block, 0)
    ext_scr[0:CONF_PAD, 0:d] = ext_scr[tm:tm + CONF_PAD, 0:d]

    y = y_scr[...]
    mu = jnp.mean(y, axis=-1, keepdims=True)
    var = jnp.mean(jnp.square(y - mu), axis=-1, keepdims=True)
    ln = (y - mu) * lax.rsqrt(var + EPS) * lg_ref[...] + lb_ref[...]
    act = (ln * jax.nn.sigmoid(ln)).astype(BF16)
    out = jnp.dot(act, w2_ref[...], preferred_element_type=F32) + b2_ref[...]
    o_ref[0] = x + gate * out


def _conf_call(x, mod, norm_g, w_pw1_all, b_pw1, dw_w, dw_b, ln_g, ln_b, w_pw2_all, b_pw2, layer):
    b, s, d = x.shape
    kw = dw_w.shape[0]
    tm, rb, cw = TM_CONF, 128, 128
    assert s % tm == 0 and kw - 1 <= CONF_PAD
    row_spec = pl.BlockSpec((1, tm, d), lambda i, j: (i, j, 0))
    n = tm + CONF_PAD
    return pl.pallas_call(
        functools.partial(_conf_kernel, tm=tm, d=d, kw=kw, rb=rb, cw=cw),
        grid=(b, s // tm),
        in_specs=[
            row_spec,
            pl.BlockSpec((1, 6, d), lambda i, j: (i, 0, 0)),
            _resident((1, d)),
            _resident_layer(w_pw1_all, layer),
            _resident((1, 2 * d)),
            _resident((kw, d)),
            _resident((1, d)),
            _resident((1, d)),
            _resident((1, d)),
            _resident_layer(w_pw2_all, layer),
            _resident((1, d)),
        ],
        out_specs=row_spec,
        out_shape=jax.ShapeDtypeStruct((b, s, d), F32),
        scratch_shapes=[
            pltpu.VMEM((8, n + CONF_ROW_PAD, d + CONF_LANE_PAD), F32),
            pltpu.VMEM((tm, d), F32),
            pltpu.VMEM((kw, 8, d), F32),
        ],
        compiler_params=_params(2),
        name="conformer_conv",
    )(x, mod, norm_g.reshape(1, d), w_pw1_all, b_pw1.reshape(1, -1), dw_w, dw_b.reshape(1, d),
      ln_g.reshape(1, d), ln_b.reshape(1, d), w_pw2_all, b_pw2.reshape(1, d))


def _split3(x):
    hi = x.astype(BF16)
    rest = x - hi.astype(F32)
    mid = rest.astype(BF16)
    lo = (rest - mid.astype(F32)).astype(BF16)
    return hi, mid, lo


def _gate_factor_constants(tm, seg):
    gl, nh, n = GATE_GROUP, MLSTM_HEADS, GATE_LANES
    col_place = np.zeros((3, n, n), np.float32)
    row_place = np.zeros((3, n, n), np.float32)
    col_ones = np.zeros((1, n), np.float32)
    row_ones = np.zeros((n, n), np.float32)
    expand = np.zeros((n, 2 * n), np.float32)
    for h in range(nh):
        for j in range(3):
            col_place[j, h, gl * h + j] = 1.0
            col_place[j, nh + h, gl * h + 9 + j] = 1.0
            row_place[j, gl * h + 3 + j, nh + h] = 1.0
            row_place[j, gl * h + 6 + j, h] = -1.0
            col_ones[0, gl * h + 3 + j] = 1.0
            col_ones[0, gl * h + 6 + j] = 1.0
            row_ones[gl * h + j, :] = 1.0
            expand[gl * h + j, :n] = 1.0
            expand[gl * h + j, n:] = -1.0
            expand[gl * h + 9 + j, n:] = 1.0
    pos = np.arange(tm)
    tri = ((pos[:, None] >= pos[None, :]) & (pos[:, None] // seg == pos[None, :] // seg))
    as_bf16 = lambda a: jnp.asarray(a, BF16)
    return (as_bf16(col_place), as_bf16(row_place), jnp.asarray(col_ones), jnp.asarray(row_ones),
            as_bf16(expand), as_bf16(tri.astype(np.float32)))


def _mfront_kernel(x_ref, mod_ref, ng_ref, wup_ref, cw_ref, cb_ref, wq_ref, wkt_ref, wv_ref,
                   wgc_ref, wgm_ref, bif_ref, cplace_ref, rplace_ref, cones_ref, rones_ref, tri_ref,
                   q_ref, kt_ref, v_ref, xc_ref, z_ref, gcol_ref, grow_ref,
                   h_scr, halo_scr, *, tm, inner, nc, k_scale):
    n_heads = MLSTM_HEADS

    @pl.when(pl.program_id(1) == 0)
    def _():
        halo_scr[...] = jnp.zeros_like(halo_scr)

    x = x_ref[0]
    shift, scale = mod_ref[0, 0:1, :], mod_ref[0, 1:2, :]
    h_scr[...] = _norm_modulate(x, ng_ref[...], shift, scale).astype(BF16)

    kw = cw_ref.shape[0]
    gates = jnp.broadcast_to(bif_ref[...], (tm, GATE_LANES))
    n_chunks = inner // nc
    xm_next = jnp.dot(h_scr[...], wup_ref[:, 0:nc], preferred_element_type=F32)
    for c in range(n_chunks):
        cols = slice(c * nc, (c + 1) * nc)
        xm = xm_next
        if c + 1 < n_chunks:
            xm_next = jnp.dot(h_scr[...], wup_ref[:, (c + 1) * nc:(c + 2) * nc],
                              preferred_element_type=F32)
        prev8 = halo_scr[:, cols]
        halo_scr[:, cols] = xm[tm - 8:tm]
        taps = tuple(cw_ref[kw - 1 - j:kw - j, cols] for j in range(kw))
        y = _causal_conv_rows(xm, prev8, taps, cb_ref[0:1, cols])
        xm_b = xm.astype(BF16)
        xc_b = (y * jax.nn.sigmoid(y)).astype(BF16)
        xc_ref[0, :, cols] = xc_b

        for g in range(c * nc // BD_GROUP, (c + 1) * nc // BD_GROUP):
            gcols = slice(g * BD_GROUP, (g + 1) * BD_GROUP)
            sub = slice(g * BD_GROUP - c * nc, (g + 1) * BD_GROUP - c * nc)
            xc_g, xm_g = xc_b[:, sub], xm_b[:, sub]
            q = jnp.dot(xc_g, wq_ref[g], preferred_element_type=F32)
            v = jnp.dot(xm_g, wv_ref[g], preferred_element_type=F32)
            k_t = lax.dot_general(wkt_ref[g], xc_g, (((1,), (1,)), ((), ())),
                                  preferred_element_type=F32)
            gates = gates + jnp.dot(xc_g, wgc_ref[g], preferred_element_type=F32)
            gates = gates + jnp.dot(xm_g, wgm_ref[g], preferred_element_type=F32)
            q_ref[0, :, gcols] = q.astype(BF16)
            kt_ref[0, gcols, :] = (k_t * k_scale).astype(BF16)
            v_ref[0, :, gcols] = v.astype(BF16)

        zcols = slice(inner + c * nc, inner + (c + 1) * nc)
        z_ref[0, :, cols] = jnp.dot(h_scr[...], wup_ref[:, zcols],
                                    preferred_element_type=F32).astype(BF16)

    lane = lax.broadcasted_iota(jnp.int32, gates.shape, 1)
    log_f = jax.nn.log_sigmoid(gates)
    csum = jnp.zeros((tm, GATE_LANES), F32)
    for piece in _split3(log_f):
        csum = csum + jnp.dot(tri_ref[...], piece, preferred_element_type=F32)
    gi = jnp.where(lane < n_heads, csum, gates)
    gcol = jnp.broadcast_to(cones_ref[...], (tm, GATE_LANES))
    grow = jnp.concatenate([rones_ref[...]] * (tm // GATE_LANES), axis=1)
    for j, piece in enumerate(_split3(gi)):
        gcol = gcol + jnp.dot(piece, cplace_ref[j], preferred_element_type=F32)
        grow = grow + lax.dot_general(rplace_ref[j], piece, (((1,), (1,)), ((), ())),
                                      preferred_element_type=F32)
    gcol_ref[0] = gcol.astype(BF16)
    grow_ref[0] = grow.astype(BF16)


def _qkv_prep_kernel(wq_ref, wk_ref, wkt_ref, wv_ref, wif_ref, dq_ref, dkt_ref, dv_ref, gc_ref,
                     gm_ref):
    row = lax.broadcasted_iota(jnp.int32, (BD_GROUP, BD_GROUP), 0)
    col = lax.broadcasted_iota(jnp.int32, (BD_GROUP, BD_GROUP), 1)
    same_block = (row // QKV_BLOCK) == (col // QKV_BLOCK)

    def dense(w_ref):
        w = w_ref[...]
        tiled = jnp.zeros((BD_GROUP, BD_GROUP), F32)
        for o in range(QKV_BLOCK):
            tiled = jnp.where(col % QKV_BLOCK == o, w[:, o:o + 1], tiled)
        return jnp.where(same_block, tiled, 0.0)

    dq, dk, dv = dense(wq_ref), dense(wk_ref), dense(wv_ref)
    dq_ref[...] = dq.astype(BF16)
    dkt_ref[...] = dense(wkt_ref).astype(BF16)
    dv_ref[...] = dv.astype(BF16)
    hi = lax.Precision.HIGHEST
    gc = (jnp.dot(dq, wif_ref[0], preferred_element_type=F32, precision=hi)
          + jnp.dot(dk, wif_ref[1], preferred_element_type=F32, precision=hi))
    gc_ref[...] = gc.astype(BF16)
    gm_ref[...] = jnp.dot(dv, wif_ref[2], preferred_element_type=F32, precision=hi).astype(BF16)


def _qkv_prep_call(wq, wk, wv, w_if):
    n_layers, n_blocks = wq.shape[0], wq.shape[1]
    n_heads = MLSTM_HEADS
    inner = n_blocks * QKV_BLOCK
    n_groups = inner // BD_GROUP
    rows = lambda w: w.reshape(n_layers, n_groups, BD_GROUP, QKV_BLOCK)
    w_if4 = w_if.reshape(n_layers, 3, inner, 2 * n_heads)
    w_if4 = jnp.concatenate([w_if4[..., n_heads:], w_if4[..., :n_heads]], axis=-1)
    w_if4 = jnp.pad(w_if4, ((0, 0), (0, 0), (0, 0), (0, GATE_LANES - 2 * n_heads)))
    w_spec = pl.BlockSpec((None, None, BD_GROUP, QKV_BLOCK), lambda l, g: (l, g, 0, 0))
    dense_spec = pl.BlockSpec((None, None, BD_GROUP, BD_GROUP), lambda l, g: (l, g, 0, 0))
    gate_spec = pl.BlockSpec((None, None, BD_GROUP, GATE_LANES), lambda l, g: (l, g, 0, 0))
    dense_shape = jax.ShapeDtypeStruct((n_layers, n_groups, BD_GROUP, BD_GROUP), BF16)
    gate_shape = jax.ShapeDtypeStruct((n_layers, n_groups, BD_GROUP, GATE_LANES), BF16)
    return pl.pallas_call(
        _qkv_prep_kernel,
        grid=(n_layers, n_groups),
        in_specs=[w_spec, w_spec, w_spec, w_spec,
                  pl.BlockSpec((None, 3, BD_GROUP, GATE_LANES), lambda l, g: (l, 0, g, 0))],
        out_specs=[dense_spec, dense_spec, dense_spec, gate_spec, gate_spec],
        out_shape=[dense_shape, dense_shape, dense_shape, gate_shape, gate_shape],
        compiler_params=_params(2),
        name="qkv_weight_prep",
    )(rows(wq), rows(wk), rows(jnp.swapaxes(wk, -1, -2)), rows(wv), w_if4)


def _mfront_call(x, mod, norm_g, w_up_all, conv_w, conv_b, qkv_weights, b_if, layer, tm=TM_MFRONT):
    b, s, d = x.shape
    inner = w_up_all.shape[2] // 2
    n_heads = MLSTM_HEADS
    assert s % tm == 0 and tm % CHUNK_LEN == 0 and 2 * n_heads == 8
    dh = inner // n_heads
    b_if2 = jnp.concatenate([b_if[n_heads:], b_if[:n_heads]])
    b_if2 = jnp.pad(b_if2, (0, GATE_LANES - 2 * n_heads)).reshape(1, GATE_LANES)
    col_place, row_place, col_ones, row_ones, _, tri = _gate_factor_constants(tm, CHUNK_LEN)

    row_spec = pl.BlockSpec((1, tm, d), lambda i, j: (i, j, 0))
    wide_spec = pl.BlockSpec((1, tm, inner), lambda i, j: (i, j, 0))
    wide_shape = jax.ShapeDtypeStruct((b, s, inner), BF16)
    return pl.pallas_call(
        functools.partial(_mfront_kernel, tm=tm, inner=inner, nc=512, k_scale=float(dh) ** -0.5),
        grid=(b, s // tm),
        in_specs=[
            row_spec,
            pl.BlockSpec((1, 6, d), lambda i, j: (i, 0, 0)),
            _resident((1, d)),
            _resident_layer(w_up_all, layer),
            _resident(conv_w.shape),
            _resident((1, inner)),
            *[_resident_layer(w, layer) for w in qkv_weights],
            _resident((1, GATE_LANES)),
            _resident((3, GATE_LANES, GATE_LANES)),
            _resident((3, GATE_LANES, GATE_LANES)),
            _resident((1, GATE_LANES)),
            _resident((GATE_LANES, GATE_LANES)),
            _resident((tm, tm)),
        ],
        out_specs=[
            wide_spec,
            pl.BlockSpec((1, inner, tm), lambda i, j: (i, 0, j)),
            wide_spec, wide_spec, wide_spec,
            pl.BlockSpec((1, tm, GATE_LANES), lambda i, j: (i, j, 0)),
            pl.BlockSpec((1, GATE_LANES, tm), lambda i, j: (i, 0, j)),
        ],
        out_shape=[
            wide_shape,
            jax.ShapeDtypeStruct((b, inner, s), BF16),
            wide_shape, wide_shape, wide_shape,
            jax.ShapeDtypeStruct((b, s, GATE_LANES), BF16),
            jax.ShapeDtypeStruct((b, GATE_LANES, s), BF16),
        ],
        scratch_shapes=[
            pltpu.VMEM((tm, d), BF16),
            pltpu.VMEM((8, inner), F32),
        ],
        compiler_params=_params(2),
        name="mlstm_front",
    )(x, mod, norm_g.reshape(1, d), w_up_all, conv_w, conv_b.reshape(1, inner), *qkv_weights,
      b_if2, col_place, row_place, col_ones, row_ones, tri)


def _lanes(a, n):
    return jnp.concatenate([a] * n, axis=1)


def _mchunk_kernel(q_ref, kt_ref, v_ref, xc_ref, z_ref, gcol_ref, grow_ref, expand_ref,
                   x_ref, mod_ref, lnw_ref, skip_ref, wdn_ref, o_ref, c_scr, m_scr, hn_scr, *,
                   ln, dh):
    n_heads = MLSTM_HEADS
    rep = GATE_LANES
    n_rep = dh // rep

    @pl.when(pl.program_id(1) == 0)
    def _():
        c_scr[...] = jnp.zeros_like(c_scr)
        m_scr[...] = jnp.zeros_like(m_scr)

    causal = (lax.broadcasted_iota(jnp.int32, (ln, ln), 0)
              >= lax.broadcasted_iota(jnp.int32, (ln, ln), 1))
    group = lax.broadcasted_iota(jnp.int32, (ln, rep), 1) // GATE_GROUP
    gcol = gcol_ref[0]
    for h in range(n_heads):
        cols = slice(h * dh, (h + 1) * dh)
        q, k_t, v = q_ref[0, :, cols], kt_ref[0, cols, :], v_ref[0, :, cols]
        gcol_h = jnp.where(group == h, gcol, jnp.zeros_like(gcol))
        dmat = jnp.dot(gcol_h, grow_ref[0], preferred_element_type=F32)
        gi = jnp.dot(gcol_h, expand_ref[...], preferred_element_type=F32)
        g, i_minus_g = gi[:, 0:rep], gi[:, rep:2 * rep]
        m = m_scr[h, 0:1, :]
        g_tot = g[ln - 1:ln, :]

        dmat = jnp.where(causal, dmat, -jnp.inf)
        inter = m + g
        m_row = jnp.maximum(inter, jnp.max(dmat, axis=-1, keepdims=True))
        w_intra = jnp.exp(dmat - _lanes(m_row, ln // rep))
        w_inter = jnp.exp(inter - m_row)
        s = jnp.dot(q, k_t, preferred_element_type=F32) * w_intra
        c_t = c_scr[h]
        qc = jnp.dot(q, c_t.astype(BF16), preferred_element_type=F32)
        num = (_lanes(w_inter, n_rep) * qc[:, 0:dh]
               + jnp.dot(s.astype(BF16), v, preferred_element_type=F32))
        den = w_inter * qc[:, dh:dh + rep] + jnp.sum(s, axis=-1, keepdims=True)
        inv = 1.0 / jnp.maximum(jnp.abs(den), jnp.exp(-m_row))
        hh = num * _lanes(inv, n_rep)
        mu = jnp.mean(hh, axis=-1, keepdims=True)
        var = jnp.mean(jnp.square(hh - mu), axis=-1, keepdims=True)
        hn_scr[:, cols] = (hh - mu) * lax.rsqrt(var + EPS)

        decay = i_minus_g + g_tot
        m_new = jnp.maximum(m + g_tot, jnp.max(decay, axis=0, keepdims=True))
        w_s = jnp.exp(decay - m_new)
        w_c = jnp.exp(m + g_tot - m_new)
        vw = jnp.concatenate(
            [(_lanes(w_s, n_rep) * v.astype(F32)).astype(BF16), w_s.astype(BF16)], axis=1)
        d_c = jnp.dot(k_t, vw, preferred_element_type=F32)
        c_scr[h] = _lanes(w_c, n_rep + 1) * c_t + d_c
        m_scr[h] = jnp.broadcast_to(m_new, m_scr.shape[1:])

    z = z_ref[0].astype(F32)
    pre = ((hn_scr[...] * lnw_ref[...] + skip_ref[...] * xc_ref[0].astype(F32))
           * (z * jax.nn.sigmoid(z)))
    y = jnp.dot(pre.astype(BF16), wdn_ref[...], preferred_element_type=F32)
    o_ref[0] = x_ref[0] + mod_ref[0, 2:3, :] * y


def _mchunk_call(q, k_t, v, xc, z, gcol, grow, x, mod, ln_w, skip, w_down_all, layer):
    b, s, d = x.shape
    inner = q.shape[-1]
    n_heads = MLSTM_HEADS
    dh = inner // n_heads
    ln = CHUNK_LEN
    wide_spec = pl.BlockSpec((1, ln, inner), lambda i, j: (i, j, 0))
    row_spec = pl.BlockSpec((1, ln, d), lambda i, j: (i, j, 0))
    return pl.pallas_call(
        functools.partial(_mchunk_kernel, ln=ln, dh=dh),
        grid=(b, s // ln),
        in_specs=[
            wide_spec,
            pl.BlockSpec((1, inner, ln), lambda i, j: (i, 0, j)),
            wide_spec, wide_spec, wide_spec,
            pl.BlockSpec((1, ln, GATE_LANES), lambda i, j: (i, j, 0)),
            pl.BlockSpec((1, GATE_LANES, ln), lambda i, j: (i, 0, j)),
            _resident((GATE_LANES, 2 * GATE_LANES)),
            row_spec,
            pl.BlockSpec((1, 6, d), lambda i, j: (i, 0, 0)),
            _resident((1, inner)),
            _resident((1, inner)),
            _resident_layer(w_down_all, layer),
        ],
        out_specs=row_spec,
        out_shape=jax.ShapeDtypeStruct((b, s, d), F32),
        scratch_shapes=[
            pltpu.VMEM((n_heads, dh, dh + GATE_LANES), F32),
            pltpu.VMEM((n_heads, 8, GATE_LANES), F32),
            pltpu.VMEM((ln, inner), F32),
        ],
        compiler_params=_params(2),
        name="mlstm_chunk",
    )(q, k_t, v, xc, z, gcol, grow, _gate_factor_constants(ln, ln)[4], x, mod,
      ln_w.reshape(1, inner), skip.reshape(1, inner), w_down_all)


def kernel(x, c, ada_w, ada_b, norm_mix_g, norm_ffn_g, final_g, mlstm_w_up, mlstm_conv_w,
           mlstm_conv_b, mlstm_wq, mlstm_wk, mlstm_wv, mlstm_w_if, mlstm_b_if, mlstm_ln_w,
           mlstm_skip, mlstm_w_down, conf_w_pw1, conf_b_pw1, conf_dw_w, conf_dw_b, conf_ln_g,
           conf_ln_b, conf_w_pw2, conf_b_pw2, ffn_w_up, ffn_conv_w, ffn_conv_b, ffn_w_down):
    depth = ada_w.shape[0]
    b, _, d = x.shape
    mods = _mods_call(c, ada_w, ada_b).reshape(depth, b, 6, d)
    mlstm_w_up, mlstm_w_down = mlstm_w_up.astype(BF16), mlstm_w_down.astype(BF16)
    conf_w_pw1, conf_w_pw2 = conf_w_pw1.astype(BF16), conf_w_pw2.astype(BF16)
    ffn_w_up, ffn_w_down = ffn_w_up.astype(BF16), ffn_w_down.astype(BF16)
    qkv_weights = _qkv_prep_call(mlstm_wq, mlstm_wk, mlstm_wv, mlstm_w_if)
    for layer in range(depth):
        mod = mods[layer]
        j = layer // 2
        if layer % 2 == 0:
            q, k_t, v, xc, z, gcol, grow = _mfront_call(
                x, mod, norm_mix_g[layer], mlstm_w_up, mlstm_conv_w[j], mlstm_conv_b[j],
                qkv_weights, mlstm_b_if[j], j)
            x = _mchunk_call(q, k_t, v, xc, z, gcol, grow, x, mod, mlstm_ln_w[j], mlstm_skip[j],
                             mlstm_w_down, j)
        else:
            x = _conf_call(x, mod, norm_mix_g[layer], conf_w_pw1, conf_b_pw1[j], conf_dw_w[j],
                           conf_dw_b[j], conf_ln_g[j], conf_ln_b[j], conf_w_pw2, conf_b_pw2[j], j)
        x = _ffn_call(x, mod, norm_ffn_g[layer], ffn_w_up, ffn_conv_w[layer], ffn_conv_b[layer],
                      ffn_w_down, layer, final_g if layer == depth - 1 else None)
    return x
```

```python
import functools

import jax
import jax.numpy as jnp
import numpy as np
from jax import lax
from jax.experimental import pallas as pl
from jax.experimental.pallas import tpu as pltpu

F32 = jnp.float32
BF16 = jnp.bfloat16
EPS = 1e-6

MLSTM_HEADS = 4
QKV_BLOCK = 4
GATE_LANES = 128
GATE_GROUP = 16
BD_GROUP = 256

TM_FFN = 1024
TM_CONF = 512
TM_MFRONT = 512
CHUNK_LEN = 256

VMEM_LIMIT_BYTES = 56 * 1024 * 1024


def _params(n_axes):
    return pltpu.CompilerParams(
        dimension_semantics=("arbitrary",) * n_axes,
        vmem_limit_bytes=VMEM_LIMIT_BYTES,
    )


def _resident(shape):
    zeros = (0,) * len(shape)
    return pl.BlockSpec(shape, lambda *_: zeros, pipeline_mode=pl.Buffered(1))


def _resident_layer(stacked, layer):
    shape = stacked.shape[1:]
    index = (layer,) + (0,) * len(shape)
    return pl.BlockSpec((None,) + shape, lambda *_: index, pipeline_mode=pl.Buffered(1))


def _rmsnorm(x, g):
    return x * lax.rsqrt(jnp.mean(x * x, axis=-1, keepdims=True) + EPS) * g


def _norm_modulate(x, g, shift, scale):
    return _rmsnorm(x, g) * (1.0 + scale) + shift


def _causal_conv_rows(u, prev8, taps, bias):
    top = u[0:8]
    row8 = lax.broadcasted_iota(jnp.int32, top.shape, 0)
    y = taps[0] * u + bias
    y_top = taps[0] * top + bias
    for j in range(1, len(taps)):
        y = y + taps[j] * pltpu.roll(u, j, 0)
        shifted_top = jnp.where(row8 < j, pltpu.roll(prev8, j, 0), pltpu.roll(top, j, 0))
        y_top = y_top + taps[j] * shifted_top
    return jnp.concatenate([y_top, y[8:]], axis=0)


def _mods_kernel(c_ref, w_ref, b_ref, o_ref):
    c = c_ref[...]
    c_act = (c * jax.nn.sigmoid(c)).astype(BF16)
    o_ref[0] = jnp.dot(c_act, w_ref[0].astype(BF16), preferred_element_type=F32) + b_ref[0]


def _mods_call(c, ada_w, ada_b):
    depth, d, n = ada_w.shape
    b = c.shape[0]
    tn = 1024
    return pl.pallas_call(
        _mods_kernel,
        grid=(depth, n // tn),
        in_specs=[
            pl.BlockSpec((b, d), lambda l, j: (0, 0)),
            pl.BlockSpec((1, d, tn), lambda l, j: (l, 0, j)),
            pl.BlockSpec((1, 1, tn), lambda l, j: (l, 0, j)),
        ],
        out_specs=pl.BlockSpec((1, b, tn), lambda l, j: (l, 0, j)),
        out_shape=jax.ShapeDtypeStruct((depth, b, n), F32),
        compiler_params=_params(2),
        name="adaln_mods",
    )(c, ada_w, ada_b.reshape(depth, 1, n))


def _ffn_kernel(x_ref, mod_ref, ng_ref, wup_ref, cw_ref, cb_ref, wdn_ref, *rest, tm, f_dim, fc,
                final):
    if final:
        fg_ref, o_ref, h_scr, act_scr, halo_scr = rest
    else:
        o_ref, h_scr, act_scr, halo_scr = rest

    @pl.when(pl.program_id(1) == 0)
    def _():
        halo_scr[...] = jnp.zeros_like(halo_scr)

    x = x_ref[0]
    shift, scale, gate = mod_ref[0, 3:4, :], mod_ref[0, 4:5, :], mod_ref[0, 5:6, :]
    h_scr[...] = _norm_modulate(x, ng_ref[...], shift, scale).astype(BF16)

    for f in range(f_dim // fc):
        halves = []
        for half in range(2):
            c0 = half * f_dim + f * fc
            cols = slice(c0, c0 + fc)
            u = jnp.dot(h_scr[...], wup_ref[:, cols], preferred_element_type=F32)
            prev8 = halo_scr[:, cols]
            halo_scr[:, cols] = u[tm - 8:tm]
            taps = (cw_ref[2:3, cols], cw_ref[1:2, cols], cw_ref[0:1, cols])
            halves.append(_causal_conv_rows(u, prev8, taps, cb_ref[0:1, cols]))
        yg, yv = halves
        act_scr[:, f * fc:(f + 1) * fc] = (yg * jax.nn.sigmoid(yg) * yv).astype(BF16)

    y = jnp.dot(act_scr[...], wdn_ref[...], preferred_element_type=F32)
    out = x + gate * y
    if final:
        out = _rmsnorm(out, fg_ref[...])
    o_ref[0] = out


def _ffn_call(x, mod, norm_g, w_up_all, conv_w, conv_b, w_down_all, layer, final_g=None,
              tm=TM_FFN, fc=256):
    b, s, d = x.shape
    f_dim = w_down_all.shape[1]
    assert s % tm == 0 and f_dim % fc == 0
    final = final_g is not None
    row_spec = pl.BlockSpec((1, tm, d), lambda i, j: (i, j, 0))
    in_specs = [
        row_spec,
        pl.BlockSpec((1, 6, d), lambda i, j: (i, 0, 0)),
        _resident((1, d)),
        _resident_layer(w_up_all, layer),
        _resident((3, 2 * f_dim)),
        _resident((1, 2 * f_dim)),
        _resident_layer(w_down_all, layer),
    ]
    args = [x, mod, norm_g.reshape(1, d), w_up_all, conv_w, conv_b.reshape(1, -1), w_down_all]
    if final:
        in_specs.append(_resident((1, d)))
        args.append(final_g.reshape(1, d))
    return pl.pallas_call(
        functools.partial(_ffn_kernel, tm=tm, f_dim=f_dim, fc=fc, final=final),
        grid=(b, s // tm),
        in_specs=in_specs,
        out_specs=row_spec,
        out_shape=jax.ShapeDtypeStruct((b, s, d), F32),
        scratch_shapes=[
            pltpu.VMEM((tm, d), BF16),
            pltpu.VMEM((tm, f_dim), BF16),
            pltpu.VMEM((8, 2 * f_dim), F32),
        ],
        compiler_params=_params(2),
        name="conv_ffn",
    )(*args)


CONF_PAD = 32
CONF_LANE_PAD = 128
CONF_ROW_PAD = 8


def _conf_kernel(x_ref, mod_ref, ng_ref, w1_ref, b1_ref, dw_ref, db_ref, lg_ref, lb_ref, w2_ref,
                 b2_ref, o_ref, stage_scr, y_scr, taps_scr, *, tm, d, kw, rb, cw):
    n = tm + CONF_PAD
    ext_scr = stage_scr.at[0]

    @pl.when(pl.program_id(1) == 0)
    def _():
        ext_scr[0:CONF_PAD, 0:d] = jnp.zeros((CONF_PAD, d), F32)

    @pl.when((pl.program_id(0) == 0) & (pl.program_id(1) == 0))
    def _():
        for k in range(kw):
            taps_scr[k] = jnp.broadcast_to(dw_ref[k:k + 1, :], (8, d))

    x = x_ref[0]
    shift, scale, gate = mod_ref[0, 0:1, :], mod_ref[0, 1:2, :], mod_ref[0, 2:3, :]
    h = _norm_modulate(x, ng_ref[...], shift, scale).astype(BF16)
    a = jnp.dot(h, w1_ref[:, 0:d], preferred_element_type=F32) + b1_ref[0:1, 0:d]
    g = jnp.dot(h, w1_ref[:, d:2 * d], preferred_element_type=F32) + b1_ref[0:1, d:2 * d]
    ext_scr[CONF_PAD:CONF_PAD + tm, 0:d] = a * jax.nn.sigmoid(g)

    ext = ext_scr[0:n, 0:d]
    for b in range(1, 8):
        stage_scr[b, 0:n - 8, 0:d] = pltpu.roll(ext, n - b, 0)[0:n - 8]

    base = CONF_PAD - (kw - 1)
    for c in range(d // cw):
        cols = slice(c * cw, (c + 1) * cw)
        taps = [jnp.concatenate([taps_scr[k, :, cols]] * (rb // 8), axis=0) for k in range(kw)]
        bias = jnp.broadcast_to(db_ref[0:1, cols], (rb, cw))

        def row_block(i, carry, cols=cols, taps=taps, bias=bias):
            r0 = pl.multiple_of(i * rb, rb)
            acc = bias
            for k in range(kw):
                q8, b = divmod(base + k, 8)
                acc = acc + taps[k] * stage_scr[b, pl.ds(r0 + 8 * q8, rb), cols]
            y_scr[pl.ds(r0, rb), cols] = acc
            return carry

        lax.fori_loop(0, tm // rb, row_block, 0)
    ext_scr[0:CONF_PAD, 0:d] = ext_scr[tm:tm + CONF_PAD, 0:d]

    y = y_scr[...]
    mu = jnp.mean(y, axis=-1, keepdims=True)
    var = jnp.mean(jnp.square(y - mu), axis=-1, keepdims=True)
    ln = (y - mu) * lax.rsqrt(var + EPS) * lg_ref[...] + lb_ref[...]
    act = (ln * jax.nn.sigmoid(ln)).astype(BF16)
    out = jnp.dot(act, w2_ref[...], preferred_element_type=F32) + b2_ref[...]
    o_ref[0] = x + gate * out


def _conf_call(x, mod, norm_g, w_pw1_all, b_pw1, dw_w, dw_b, ln_g, ln_b, w_pw2_all, b_pw2, layer):
    b, s, d = x.shape
    kw = dw_w.shape[0]
    tm, rb, cw = TM_CONF, 128, 128
    assert s % tm == 0 and kw - 1 <= CONF_PAD
    row_spec = pl.BlockSpec((1, tm, d), lambda i, j: (i, j, 0))
    n = tm + CONF_PAD
    return pl.pallas_call(
        functools.partial(_conf_kernel, tm=tm, d=d, kw=kw, rb=rb, cw=cw),
        grid=(b, s // tm),
        in_specs=[
            row_spec,
            pl.BlockSpec((1, 6, d), lambda i, j: (i, 0, 0)),
            _resident((1, d)),
            _resident_layer(w_pw1_all, layer),
            _resident((1, 2 * d)),
            _resident((kw, d)),
            _resident((1, d)),
            _resident((1, d)),
            _resident((1, d)),
            _resident_layer(w_pw2_all, layer),
            _resident((1, d)),
        ],
        out_specs=row_spec,
        out_shape=jax.ShapeDtypeStruct((b, s, d), F32),
        scratch_shapes=[
            pltpu.VMEM((8, n + CONF_ROW_PAD, d + CONF_LANE_PAD), F32),
            pltpu.VMEM((tm, d), F32),
            pltpu.VMEM((kw, 8, d), F32),
        ],
        compiler_params=_params(2),
        name="conformer_conv",
    )(x, mod, norm_g.reshape(1, d), w_pw1_all, b_pw1.reshape(1, -1), dw_w, dw_b.reshape(1, d),
      ln_g.reshape(1, d), ln_b.reshape(1, d), w_pw2_all, b_pw2.reshape(1, d))


def _split3(x):
    hi = x.astype(BF16)
    rest = x - hi.astype(F32)
    mid = rest.astype(BF16)
    lo = (rest - mid.astype(F32)).astype(BF16)
    return hi, mid, lo


def _gate_factor_constants(tm, seg):
    gl, nh, n = GATE_GROUP, MLSTM_HEADS, GATE_LANES
    col_place = np.zeros((3, n, n), np.float32)
    row_place = np.zeros((3, n, n), np.float32)
    col_ones = np.zeros((1, n), np.float32)
    row_ones = np.zeros((n, n), np.float32)
    expand = np.zeros((n, 2 * n), np.float32)
    for h in range(nh):
        for j in range(3):
            col_place[j, h, gl * h + j] = 1.0
            col_place[j, nh + h, gl * h + 9 + j] = 1.0
            row_place[j, gl * h + 3 + j, nh + h] = 1.0
            row_place[j, gl * h + 6 + j, h] = -1.0
            col_ones[0, gl * h + 3 + j] = 1.0
            col_ones[0, gl * h + 6 + j] = 1.0
            row_ones[gl * h + j, :] = 1.0
            expand[gl * h + j, :n] = 1.0
            expand[gl * h + j, n:] = -1.0
            expand[gl * h + 9 + j, n:] = 1.0
    pos = np.arange(tm)
    tri = ((pos[:, None] >= pos[None, :]) & (pos[:, None] // seg == pos[None, :] // seg))
    as_bf16 = lambda a: jnp.asarray(a, BF16)
    return (as_bf16(col_place), as_bf16(row_place), jnp.asarray(col_ones), jnp.asarray(row_ones),
            as_bf16(expand), as_bf16(tri.astype(np.float32)))


def _mfront_kernel(x_ref, mod_ref, ng_ref, wup_ref, cw_ref, cb_ref, wq_ref, wkt_ref, wv_ref,
                   wgc_ref, wgm_ref, bif_ref, cplace_ref, rplace_ref, cones_ref, rones_ref, tri_ref,
                   q_ref, kt_ref, v_ref, xc_ref, z_ref, gcol_ref, grow_ref,
                   h_scr, halo_scr, *, tm, inner, nc, k_scale):
    n_heads = MLSTM_HEADS

    @pl.when(pl.program_id(1) == 0)
    def _():
        halo_scr[...] = jnp.zeros_like(halo_scr)

    x = x_ref[0]
    shift, scale = mod_ref[0, 0:1, :], mod_ref[0, 1:2, :]
    h_scr[...] = _norm_modulate(x, ng_ref[...], shift, scale).astype(BF16)

    kw = cw_ref.shape[0]
    gates = jnp.broadcast_to(bif_ref[...], (tm, GATE_LANES))
    n_chunks = inner // nc
    xm_next = jnp.dot(h_scr[...], wup_ref[:, 0:nc], preferred_element_type=F32)
    for c in range(n_chunks):
        cols = slice(c * nc, (c + 1) * nc)
        xm = xm_next
        if c + 1 < n_chunks:
            xm_next = jnp.dot(h_scr[...], wup_ref[:, (c + 1) * nc:(c + 2) * nc],
                              preferred_element_type=F32)
        prev8 = halo_scr[:, cols]
        halo_scr[:, cols] = xm[tm - 8:tm]
        taps = tuple(cw_ref[kw - 1 - j:kw - j, cols] for j in range(kw))
        y = _causal_conv_rows(xm, prev8, taps, cb_ref[0:1, cols])
        xm_b = xm.astype(BF16)
        xc_b = (y * jax.nn.sigmoid(y)).astype(BF16)
        xc_ref[0, :, cols] = xc_b

        for g in range(c * nc // BD_GROUP, (c + 1) * nc // BD_GROUP):
            gcols = slice(g * BD_GROUP, (g + 1) * BD_GROUP)
            sub = slice(g * BD_GROUP - c * nc, (g + 1) * BD_GROUP - c * nc)
            xc_g, xm_g = xc_b[:, sub], xm_b[:, sub]
            q = jnp.dot(xc_g, wq_ref[g], preferred_element_type=F32)
            v = jnp.dot(xm_g, wv_ref[g], preferred_element_type=F32)
            k_t = lax.dot_general(wkt_ref[g], xc_g, (((1,), (1,)), ((), ())),
                                  preferred_element_type=F32)
            gates = gates + jnp.dot(xc_g, wgc_ref[g], preferred_element_type=F32)
            gates = gates + jnp.dot(xm_g, wgm_ref[g], preferred_element_type=F32)
            q_ref[0, :, gcols] = q.astype(BF16)
            kt_ref[0, gcols, :] = (k_t * k_scale).astype(BF16)
            v_ref[0, :, gcols] = v.astype(BF16)

        zcols = slice(inner + c * nc, inner + (c + 1) * nc)
        z_ref[0, :, cols] = jnp.dot(h_scr[...], wup_ref[:, zcols],
                                    preferred_element_type=F32).astype(BF16)

    lane = lax.broadcasted_iota(jnp.int32, gates.shape, 1)
    log_f = jax.nn.log_sigmoid(gates)
    csum = jnp.zeros((tm, GATE_LANES), F32)
    for piece in _split3(log_f):
        csum = csum + jnp.dot(tri_ref[...], piece, preferred_element_type=F32)
    gi = jnp.where(lane < n_heads, csum, gates)
    gcol = jnp.broadcast_to(cones_ref[...], (tm, GATE_LANES))
    grow = jnp.concatenate([rones_ref[...]] * (tm // GATE_LANES), axis=1)
    for j, piece in enumerate(_split3(gi)):
        gcol = gcol + jnp.dot(piece, cplace_ref[j], preferred_element_type=F32)
        grow = grow + lax.dot_general(rplace_ref[j], piece, (((1,), (1,)), ((), ())),
                                      preferred_element_type=F32)
    gcol_ref[0] = gcol.astype(BF16)
    grow_ref[0] = grow.astype(BF16)


def _qkv_prep_kernel(wq_ref, wk_ref, wkt_ref, wv_ref, wif_ref, dq_ref, dkt_ref, dv_ref, gc_ref,
                     gm_ref):
    row = lax.broadcasted_iota(jnp.int32, (BD_GROUP, BD_GROUP), 0)
    col = lax.broadcasted_iota(jnp.int32, (BD_GROUP, BD_GROUP), 1)
    same_block = (row // QKV_BLOCK) == (col // QKV_BLOCK)

    def dense(w_ref):
        w = w_ref[...]
        tiled = jnp.zeros((BD_GROUP, BD_GROUP), F32)
        for o in range(QKV_BLOCK):
            tiled = jnp.where(col % QKV_BLOCK == o, w[:, o:o + 1], tiled)
        return jnp.where(same_block, tiled, 0.0)

    dq, dk, dv = dense(wq_ref), dense(wk_ref), dense(wv_ref)
    dq_ref[...] = dq.astype(BF16)
    dkt_ref[...] = dense(wkt_ref).astype(BF16)
    dv_ref[...] = dv.astype(BF16)
    hi = lax.Precision.HIGHEST
    gc = (jnp.dot(dq, wif_ref[0], preferred_element_type=F32, precision=hi)
          + jnp.dot(dk, wif_ref[1], preferred_element_type=F32, precision=hi))
    gc_ref[...] = gc.astype(BF16)
    gm_ref[...] = jnp.dot(dv, wif_ref[2], preferred_element_type=F32, precision=hi).astype(BF16)


def _qkv_prep_call(wq, wk, wv, w_if):
    n_layers, n_blocks = wq.shape[0], wq.shape[1]
    n_heads = MLSTM_HEADS
    inner = n_blocks * QKV_BLOCK
    n_groups = inner // BD_GROUP
    rows = lambda w: w.reshape(n_layers, n_groups, BD_GROUP, QKV_BLOCK)
    w_if4 = w_if.reshape(n_layers, 3, inner, 2 * n_heads)
    w_if4 = jnp.concatenate([w_if4[..., n_heads:], w_if4[..., :n_heads]], axis=-1)
    w_if4 = jnp.pad(w_if4, ((0, 0), (0, 0), (0, 0), (0, GATE_LANES - 2 * n_heads)))
    w_spec = pl.BlockSpec((None, None, BD_GROUP, QKV_BLOCK), lambda l, g: (l, g, 0, 0))
    dense_spec = pl.BlockSpec((None, None, BD_GROUP, BD_GROUP), lambda l, g: (l, g, 0, 0))
    gate_spec = pl.BlockSpec((None, None, BD_GROUP, GATE_LANES), lambda l, g: (l, g, 0, 0))
    dense_shape = jax.ShapeDtypeStruct((n_layers, n_groups, BD_GROUP, BD_GROUP), BF16)
    gate_shape = jax.ShapeDtypeStruct((n_layers, n_groups, BD_GROUP, GATE_LANES), BF16)
    return pl.pallas_call(
        _qkv_prep_kernel,
        grid=(n_layers, n_groups),
        in_specs=[w_spec, w_spec, w_spec, w_spec,
                  pl.BlockSpec((None, 3, BD_GROUP, GATE_LANES), lambda l, g: (l, 0, g, 0))],
        out_specs=[dense_spec, dense_spec, dense_spec, gate_spec, gate_spec],
        out_shape=[dense_shape, dense_shape, dense_shape, gate_shape, gate_shape],
        compiler_params=_params(2),
        name="qkv_weight_prep",
    )(rows(wq), rows(wk), rows(jnp.swapaxes(wk, -1, -2)), rows(wv), w_if4)


def _mfront_call(x, mod, norm_g, w_up_all, conv_w, conv_b, qkv_weights, b_if, layer, tm=TM_MFRONT):
    b, s, d = x.shape
    inner = w_up_all.shape[2] // 2
    n_heads = MLSTM_HEADS
    assert s % tm == 0 and tm % CHUNK_LEN == 0 and 2 * n_heads == 8
    dh = inner // n_heads
    b_if2 = jnp.concatenate([b_if[n_heads:], b_if[:n_heads]])
    b_if2 = jnp.pad(b_if2, (0, GATE_LANES - 2 * n_heads)).reshape(1, GATE_LANES)
    col_place, row_place, col_ones, row_ones, _, tri = _gate_factor_constants(tm, CHUNK_LEN)

    row_spec = pl.BlockSpec((1, tm, d), lambda i, j: (i, j, 0))
    wide_spec = pl.BlockSpec((1, tm, inner), lambda i, j: (i, j, 0))
    wide_shape = jax.ShapeDtypeStruct((b, s, inner), BF16)
    return pl.pallas_call(
        functools.partial(_mfront_kernel, tm=tm, inner=inner, nc=1024, k_scale=float(dh) ** -0.5),
        grid=(b, s // tm),
        in_specs=[
            row_spec,
            pl.BlockSpec((1, 6, d), lambda i, j: (i, 0, 0)),
            _resident((1, d)),
            _resident_layer(w_up_all, layer),
            _resident(conv_w.shape),
            _resident((1, inner)),
            *[_resident_layer(w, layer) for w in qkv_weights],
            _resident((1, GATE_LANES)),
            _resident((3, GATE_LANES, GATE_LANES)),
            _resident((3, GATE_LANES, GATE_LANES)),
            _resident((1, GATE_LANES)),
            _resident((GATE_LANES, GATE_LANES)),
            _resident((tm, tm)),
        ],
        out_specs=[
            wide_spec,
            pl.BlockSpec((1, inner, tm), lambda i, j: (i, 0, j)),
            wide_spec, wide_spec, wide_spec,
            pl.BlockSpec((1, tm, GATE_LANES), lambda i, j: (i, j, 0)),
            pl.BlockSpec((1, GATE_LANES, tm), lambda i, j: (i, 0, j)),
        ],
        out_shape=[
            wide_shape,
            jax.ShapeDtypeStruct((b, inner, s), BF16),
            wide_shape, wide_shape, wide_shape,
            jax.ShapeDtypeStruct((b, s, GATE_LANES), BF16),
            jax.ShapeDtypeStruct((b, GATE_LANES, s), BF16),
        ],
        scratch_shapes=[
            pltpu.VMEM((tm, d), BF16),
            pltpu.VMEM((8, inner), F32),
        ],
        compiler_params=_params(2),
        name="mlstm_front",
    )(x, mod, norm_g.reshape(1, d), w_up_all, conv_w, conv_b.reshape(1, inner), *qkv_weights,
      b_if2, col_place, row_place, col_ones, row_ones, tri)


def _lanes(a, n):
    return jnp.concatenate([a] * n, axis=1)


def _mchunk_kernel(q_ref, kt_ref, v_ref, xc_ref, z_ref, gcol_ref, grow_ref, expand_ref,
                   x_ref, mod_ref, lnw_ref, skip_ref, wdn_ref, o_ref, c_scr, m_scr, hn_scr, *,
                   ln, dh):
    n_heads = MLSTM_HEADS
    rep = GATE_LANES
    n_rep = dh // rep

    @pl.when(pl.program_id(1) == 0)
    def _():
        c_scr[...] = jnp.zeros_like(c_scr)
        m_scr[...] = jnp.zeros_like(m_scr)

    causal = (lax.broadcasted_iota(jnp.int32, (ln, ln), 0)
              >= lax.broadcasted_iota(jnp.int32, (ln, ln), 1))
    group = lax.broadcasted_iota(jnp.int32, (ln, rep), 1) // GATE_GROUP
    gcol = gcol_ref[0]
    for h in range(n_heads):
        cols = slice(h * dh, (h + 1) * dh)
        q, k_t, v = q_ref[0, :, cols], kt_ref[0, cols, :], v_ref[0, :, cols]
        gcol_h = jnp.where(group == h, gcol, jnp.zeros_like(gcol))
        dmat = jnp.dot(gcol_h, grow_ref[0], preferred_element_type=F32)
        gi = jnp.dot(gcol_h, expand_ref[...], preferred_element_type=F32)
        g, i_minus_g = gi[:, 0:rep], gi[:, rep:2 * rep]
        m = m_scr[h, 0:1, :]
        g_tot = g[ln - 1:ln, :]

        dmat = jnp.where(causal, dmat, -jnp.inf)
        inter = m + g
        m_row = jnp.maximum(inter, jnp.max(dmat, axis=-1, keepdims=True))
        w_intra = jnp.exp(dmat - _lanes(m_row, ln // rep))
        w_inter = jnp.exp(inter - m_row)
        s = jnp.dot(q, k_t, preferred_element_type=F32) * w_intra
        c_t = c_scr[h]
        qc = jnp.dot(q, c_t.astype(BF16), preferred_element_type=F32)
        num = (_lanes(w_inter, n_rep) * qc[:, 0:dh]
               + jnp.dot(s.astype(BF16), v, preferred_element_type=F32))
        den = w_inter * qc[:, dh:dh + rep] + jnp.sum(s, axis=-1, keepdims=True)
        inv = 1.0 / jnp.maximum(jnp.abs(den), jnp.exp(-m_row))
        hh = num * _lanes(inv, n_rep)
        mu = jnp.mean(hh, axis=-1, keepdims=True)
        var = jnp.mean(jnp.square(hh - mu), axis=-1, keepdims=True)
        hn_scr[:, cols] = (hh - mu) * lax.rsqrt(var + EPS)

        decay = i_minus_g + g_tot
        m_new = jnp.maximum(m + g_tot, jnp.max(decay, axis=0, keepdims=True))
        w_s = jnp.exp(decay - m_new)
        w_c = jnp.exp(m + g_tot - m_new)
        vw = jnp.concatenate(
            [(_lanes(w_s, n_rep) * v.astype(F32)).astype(BF16), w_s.astype(BF16)], axis=1)
        d_c = jnp.dot(k_t, vw, preferred_element_type=F32)
        c_scr[h] = _lanes(w_c, n_rep + 1) * c_t + d_c
        m_scr[h] = jnp.broadcast_to(m_new, m_scr.shape[1:])

    z = z_ref[0].astype(F32)
    pre = ((hn_scr[...] * lnw_ref[...] + skip_ref[...] * xc_ref[0].astype(F32))
           * (z * jax.nn.sigmoid(z)))
    y = jnp.dot(pre.astype(BF16), wdn_ref[...], preferred_element_type=F32)
    o_ref[0] = x_ref[0] + mod_ref[0, 2:3, :] * y


def _mchunk_call(q, k_t, v, xc, z, gcol, grow, x, mod, ln_w, skip, w_down_all, layer):
    b, s, d = x.shape
    inner = q.shape[-1]
    n_heads = MLSTM_HEADS
    dh = inner // n_heads
    ln = CHUNK_LEN
    wide_spec = pl.BlockSpec((1, ln, inner), lambda i, j: (i, j, 0))
    row_spec = pl.BlockSpec((1, ln, d), lambda i, j: (i, j, 0))
    return pl.pallas_call(
        functools.partial(_mchunk_kernel, ln=ln, dh=dh),
        grid=(b, s // ln),
        in_specs=[
            wide_spec,
            pl.BlockSpec((1, inner, ln), lambda i, j: (i, 0, j)),
            wide_spec, wide_spec, wide_spec,
            pl.BlockSpec((1, ln, GATE_LANES), lambda i, j: (i, j, 0)),
            pl.BlockSpec((1, GATE_LANES, ln), lambda i, j: (i, 0, j)),
            _resident((GATE_LANES, 2 * GATE_LANES)),
            row_spec,
            pl.BlockSpec((1, 6, d), lambda i, j: (i, 0, 0)),
            _resident((1, inner)),
            _resident((1, inner)),
            _resident_layer(w_down_all, layer),
        ],
        out_specs=row_spec,
        out_shape=jax.ShapeDtypeStruct((b, s, d), F32),
        scratch_shapes=[
            pltpu.VMEM((n_heads, dh, dh + GATE_LANES), F32),
            pltpu.VMEM((n_heads, 8, GATE_LANES), F32),
            pltpu.VMEM((ln, inner), F32),
        ],
        compiler_params=_params(2),
        name="mlstm_chunk",
    )(q, k_t, v, xc, z, gcol, grow, _gate_factor_constants(ln, ln)[4], x, mod,
      ln_w.reshape(1, inner), skip.reshape(1, inner), w_down_all)


def kernel(x, c, ada_w, ada_b, norm_mix_g, norm_ffn_g, final_g, mlstm_w_up, mlstm_conv_w,
           mlstm_conv_b, mlstm_wq, mlstm_wk, mlstm_wv, mlstm_w_if, mlstm_b_if, mlstm_ln_w,
           mlstm_skip, mlstm_w_down, conf_w_pw1, conf_b_pw1, conf_dw_w, conf_dw_b, conf_ln_g,
           conf_ln_b, conf_w_pw2, conf_b_pw2, ffn_w_up, ffn_conv_w, ffn_conv_b, ffn_w_down):
    depth = ada_w.shape[0]
    b, _, d = x.shape
    mods = _mods_call(c, ada_w, ada_b).reshape(depth, b, 6, d)
    mlstm_w_up, mlstm_w_down = mlstm_w_up.astype(BF16), mlstm_w_down.astype(BF16)
    conf_w_pw1, conf_w_pw2 = conf_w_pw1.astype(BF16), conf_w_pw2.astype(BF16)
    ffn_w_up, ffn_w_down = ffn_w_up.astype(BF16), ffn_w_down.astype(BF16)
    qkv_weights = _qkv_prep_call(mlstm_wq, mlstm_wk, mlstm_wv, mlstm_w_if)
    for layer in range(depth):
        mod = mods[layer]
        j = layer // 2
        if layer % 2 == 0:
            q, k_t, v, xc, z, gcol, grow = _mfront_call(
                x, mod, norm_mix_g[layer], mlstm_w_up, mlstm_conv_w[j], mlstm_conv_b[j],
                qkv_weights, mlstm_b_if[j], j)
            x = _mchunk_call(q, k_t, v, xc, z, gcol, grow, x, mod, mlstm_ln_w[j], mlstm_skip[j],
                             mlstm_w_down, j)
        else:
            x = _conf_call(x, mod, norm_mix_g[layer], conf_w_pw1, conf_b_pw1[j], conf_dw_w[j],
                           conf_dw_b[j], conf_ln_g[j], conf_ln_b[j], conf_w_pw2, conf_b_pw2[j], j)
        x = _ffn_call(x, mod, norm_ffn_g[layer], ffn_w_up, ffn_conv_w[layer], ffn_conv_b[layer],
                      ffn_w_down, layer, final_g if layer == depth - 1 else None)
    return x
```
